```python
import jax, jax.numpy as jnp
from jax import lax
import numpy as np

D_MODEL = 1024
BATCH = 4
SEQ = 8192
DEPTH = 1

N_META = 16
HEAD_SIZE_A = 64
N_HEADS_A = D_MODEL // HEAD_SIZE_A
WIDTH_A = N_HEADS_A * HEAD_SIZE_A
DECAY_LORA = 64
AAA_LORA = 64
GATE_LORA = 128
LN_X_EPS = 64e-5
WIDTH_B = D_MODEL
N_BLOCKS_B = 16
BLOCK_B = WIDTH_B // N_BLOCKS_B
CONV_WIDTH = 4
LRU_C = 8.0
D_FF = 4 * D_MODEL
RMS_EPS = 1e-6
RWKV_SPLITS = (WIDTH_A, 2 * WIDTH_A, 3 * WIDTH_A, 3 * WIDTH_A + DECAY_LORA, 3 * WIDTH_A + DECAY_LORA + AAA_LORA)
COLS_A = 3 * WIDTH_A + DECAY_LORA + AAA_LORA + GATE_LORA
N_IN_COLS = COLS_A + 2 * WIDTH_B + 2 * D_MODEL

kernel_name = 'hybrid_rwkv7_rglru_block'


def rmsnorm(x, g):
    xf = x.astype(jnp.float32)
    y = xf * lax.rsqrt(jnp.mean(xf * xf, axis=-1, keepdims=True) + RMS_EPS)
    return (y * g.astype(jnp.float32)).astype(x.dtype)


def token_shift(p, mu):
    prev = jnp.pad(p[:, :-1], ((0, 0), (1, 0), (0, 0)))
    return p + (prev - p) * mu


def wkv7_scan(r, w, k, v, a_vec, b_vec):
    B, _, H, N = r.shape
    xs = tuple(jnp.moveaxis(t, 1, 0) for t in (r, w, k, v, a_vec, b_vec))

    def step(S, inp):
        r_t, w_t, k_t, v_t, a_t, b_t = inp
        sa = jnp.einsum('bhvk,bhk->bhv', S, a_t)
        S = S * w_t[:, :, None, :] + sa[..., None] * b_t[:, :, None, :] + v_t[..., None] * k_t[:, :, None, :]
        y_t = jnp.einsum('bhvk,bhk->bhv', S, r_t)
        return S, y_t

    S0 = jnp.zeros((B, H, N, N), jnp.float32)
    _, y = lax.scan(step, S0, xs)
    return jnp.moveaxis(y, 0, 1)


def rwkv7_time_mix(pa, mu_shift, w0, w_decay_up, a0, w_aaa_up, w_gate_up, k_k, k_a, r_k, ln_x_w, ln_x_b):
    dt = pa.dtype
    B, T, _ = pa.shape
    f32 = jnp.float32
    xs = token_shift(pa, mu_shift)
    r, k, v, wd, ad, gd = jnp.split(xs, RWKV_SPLITS, axis=-1)
    w_log = -jax.nn.softplus(-(w0 + jnp.tanh(wd) @ w_decay_up).astype(f32)) - 0.5
    decay = jnp.exp(-jnp.exp(w_log))
    a = jax.nn.sigmoid((a0 + ad @ w_aaa_up).astype(f32))
    g = jax.nn.sigmoid(gd) @ w_gate_up
    kf = k.astype(f32)
    kk = (kf * k_k.astype(f32)).reshape(B, T, N_HEADS_A, HEAD_SIZE_A)
    kk = kk / jnp.maximum(jnp.sqrt(jnp.sum(kk * kk, axis=-1, keepdims=True)), 1e-12)
    kmod = kf * (1.0 + (a - 1.0) * k_a.astype(f32))
    heads = lambda t: t.astype(f32).reshape(B, T, N_HEADS_A, HEAD_SIZE_A)
    rh, kh, vh, ah, dh = heads(r), heads(kmod), heads(v), heads(a), heads(decay)
    y = wkv7_scan(rh, dh, kh, vh, -kk, kk * ah)
    mean = jnp.mean(y, axis=-1, keepdims=True)
    var = jnp.mean(jnp.square(y - mean), axis=-1, keepdims=True)
    y = ((y - mean) * lax.rsqrt(var + LN_X_EPS)).reshape(B, T, WIDTH_A) * ln_x_w + ln_x_b
    bonus = jnp.sum(rh * kh * r_k.astype(f32), axis=-1, keepdims=True) * vh
    y = y + bonus.reshape(B, T, WIDTH_A)
    return (y * g).astype(dt)


def _lin_combine(c1, c2):
    a1, b1 = c1
    a2, b2 = c2
    return a1 * a2, a2 * b1 + b2


def rglru_branch(xb, yb, conv_w, conv_b, lru_wa, lru_ba, lru_wx, lru_bx, lru_lambda):
    dt = xb.dtype
    B, T, _ = xb.shape
    f32 = jnp.float32
    xp = jnp.pad(xb, ((0, 0), (CONV_WIDTH - 1, 0), (0, 0)))
    xc = conv_b + sum(xp[:, j:j + T] * conv_w[j] for j in range(CONV_WIDTH))
    xh = xc.reshape(B, T, N_BLOCKS_B, BLOCK_B)
    gate_r = jax.nn.sigmoid((jnp.einsum('bthi,hij->bthj', xh, lru_wa).reshape(B, T, WIDTH_B) + lru_ba).astype(f32))
    gate_i = jax.nn.sigmoid((jnp.einsum('bthi,hij->bthj', xh, lru_wx).reshape(B, T, WIDTH_B) + lru_bx).astype(f32))
    log_a = -LRU_C * jax.nn.softplus(-lru_lambda.astype(f32)) * gate_r
    a = jnp.exp(log_a)
    b = jnp.sqrt(-jnp.expm1(2.0 * log_a)) * (gate_i * xc.astype(f32))
    _, hs = lax.associative_scan(_lin_combine, (a, b), axis=1)
    return (hs * jax.nn.gelu(yb.astype(f32))).astype(dt)


def hybrid_layer(h, norm_mix_g, w_in, mu_shift, w0, w_decay_up, a0, w_aaa_up, w_gate_up, k_k, k_a, r_k,
                 ln_x_w, ln_x_b, w_proj_a, conv_w, conv_b, lru_wa, lru_ba, lru_wx, lru_bx, lru_lambda,
                 w_proj_b, w_out, norm_ffn_g, w_ff_up, w_ff_down):
    u = rmsnorm(h, norm_mix_g)
    p = u @ w_in
    pa, xb, yb, gates = jnp.split(p, (COLS_A, COLS_A + WIDTH_B, COLS_A + 2 * WIDTH_B), axis=-1)
    ya = rwkv7_time_mix(pa, mu_shift, w0, w_decay_up, a0, w_aaa_up, w_gate_up, k_k, k_a, r_k, ln_x_w, ln_x_b) @ w_proj_a
    yr = rglru_branch(xb, yb, conv_w, conv_b, lru_wa, lru_ba, lru_wx, lru_bx, lru_lambda) @ w_proj_b
    ga, gb = jnp.split(jax.nn.sigmoid(gates), 2, axis=-1)
    h = h + ((ga * ya + gb * yr) @ w_out).astype(h.dtype)
    z = rmsnorm(h, norm_ffn_g) @ w_ff_up
    h = h + (jnp.square(jax.nn.relu(z)) @ w_ff_down).astype(h.dtype)
    return h


def setup_inputs(seed: int = 0) -> dict:
    key = jax.random.key(seed)
    ks = jax.random.split(key, 32)
    f32 = jnp.float32
    nrm = lambda k, shape, s: s * jax.random.normal(k, shape, f32)
    L = DEPTH
    lin = jnp.linspace(0.0, 1.0, WIDTH_A, dtype=f32)
    u_lam = jax.random.uniform(ks[24], (L, WIDTH_B), f32, 0.9, 0.999)
    root = u_lam ** (1.0 / LRU_C)
    return {
        'x': nrm(ks[0], (BATCH, SEQ, D_MODEL), 1.0),
        'meta_tokens': nrm(ks[1], (N_META, D_MODEL), 1.0),
        'norm_mix_g': 1.0 + nrm(ks[2], (L, D_MODEL), 0.02),
        'w_in': nrm(ks[3], (L, D_MODEL, N_IN_COLS), D_MODEL ** -0.5),
        'mu_shift': jax.random.uniform(ks[4], (L, COLS_A), f32),
        'w0': -7.0 + 5.0 * lin ** 0.85 + nrm(ks[5], (L, WIDTH_A), 0.1),
        'w_decay_up': nrm(ks[6], (L, DECAY_LORA, WIDTH_A), 0.1),
        'a0': nrm(ks[7], (L, WIDTH_A), 0.1),
        'w_aaa_up': nrm(ks[8], (L, AAA_LORA, WIDTH_A), 0.5 * AAA_LORA ** -0.5),
        'w_gate_up': nrm(ks[9], (L, GATE_LORA, WIDTH_A), GATE_LORA ** -0.5),
        'k_k': 0.85 + nrm(ks[10], (L, WIDTH_A), 0.05),
        'k_a': 1.0 + nrm(ks[11], (L, WIDTH_A), 0.05),
        'r_k': nrm(ks[12], (L, N_HEADS_A, HEAD_SIZE_A), 0.1),
        'ln_x_w': 1.0 + nrm(ks[13], (L, WIDTH_A), 0.02),
        'ln_x_b': nrm(ks[14], (L, WIDTH_A), 0.02),
        'w_proj_a': nrm(ks[15], (L, WIDTH_A, D_MODEL), WIDTH_A ** -0.5),
        'conv_w': nrm(ks[16], (L, CONV_WIDTH, WIDTH_B), CONV_WIDTH ** -0.5),
        'conv_b': nrm(ks[17], (L, WIDTH_B), 0.02),
        'lru_wa': nrm(ks[18], (L, N_BLOCKS_B, BLOCK_B, BLOCK_B), BLOCK_B ** -0.5),
        'lru_ba': nrm(ks[19], (L, WIDTH_B), 0.02),
        'lru_wx': nrm(ks[20], (L, N_BLOCKS_B, BLOCK_B, BLOCK_B), BLOCK_B ** -0.5),
        'lru_bx': nrm(ks[21], (L, WIDTH_B), 0.02),
        'lru_lambda': jnp.log(root) - jnp.log1p(-root),
        'w_proj_b': nrm(ks[22], (L, WIDTH_B, D_MODEL), WIDTH_B ** -0.5),
        'w_out': nrm(ks[23], (L, D_MODEL, D_MODEL), D_MODEL ** -0.5),
        'norm_ffn_g': 1.0 + nrm(ks[25], (L, D_MODEL), 0.02),
        'w_ff_up': nrm(ks[26], (L, D_MODEL, D_FF), D_MODEL ** -0.5),
        'w_ff_down': nrm(ks[27], (L, D_FF, D_MODEL), D_FF ** -0.5),
        'norm_final_g': 1.0 + nrm(ks[28], (D_MODEL,), 0.02),
    }


def reference(x, meta_tokens, norm_mix_g, w_in, mu_shift, w0, w_decay_up, a0, w_aaa_up, w_gate_up, k_k, k_a,
              r_k, ln_x_w, ln_x_b, w_proj_a, conv_w, conv_b, lru_wa, lru_ba, lru_wx, lru_bx, lru_lambda,
              w_proj_b, w_out, norm_ffn_g, w_ff_up, w_ff_down, norm_final_g):
    B = x.shape[0]
    meta = jnp.broadcast_to(meta_tokens.astype(x.dtype)[None], (B, N_META, D_MODEL))
    h = jnp.concatenate([meta, x], axis=1)
    layer_params = (norm_mix_g, w_in, mu_shift, w0, w_decay_up, a0, w_aaa_up, w_gate_up, k_k, k_a, r_k,
                    ln_x_w, ln_x_b, w_proj_a, conv_w, conv_b, lru_wa, lru_ba, lru_wx, lru_bx, lru_lambda,
                    w_proj_b, w_out, norm_ffn_g, w_ff_up, w_ff_down)
    for l in range(DEPTH):
        h = hybrid_layer(h, *(p[l] for p in layer_params))
    h = rmsnorm(h, norm_final_g)
    return h[:, N_META:]
```

```python
import functools
import math

import jax
import jax.numpy as jnp
from jax import lax
from jax.experimental import pallas as pl
from jax.experimental.pallas import tpu as pltpu

F32 = jnp.float32
BF16 = jnp.bfloat16

D_MODEL = 1024
N_META = 16
HEAD = 64
N_HEADS = D_MODEL // HEAD
DECAY_LORA = 64
AAA_LORA = 64
GATE_LORA = 128
COLS_A = 3 * D_MODEL + DECAY_LORA + AAA_LORA + GATE_LORA
COLS_B = 4 * D_MODEL
LN_X_EPS = 64e-5
RMS_EPS = 1e-6
LRU_C = 8.0
CONV_WIDTH = 4
EXP_M05 = math.exp(-0.5)
SQRT_2_OVER_PI = math.sqrt(2.0 / math.pi)

CHUNK = 64
TIME_TILE = 192
ROW_TILE = 384
VMEM_LIMIT = 60 * 1024 * 1024

NT_DIMS = (((1,), (1,)), ((), ()))
TN_DIMS = (((0,), (0,)), ((), ()))


def _sigmoid(x):
    return 1.0 / (1.0 + jnp.exp(-x))


def _log1p(z):
    u = 1.0 + z
    return jnp.where(u == 1.0, z, jnp.log(u) * z / jnp.where(u == 1.0, 1.0, u - 1.0))


def _dot(a, b):
    return jnp.dot(a.astype(BF16), b.astype(BF16), preferred_element_type=F32)


def _dot_nt(a, b):
    return lax.dot_general(a.astype(BF16), b.astype(BF16), NT_DIMS, preferred_element_type=F32)


def _dot_tn(a, b):
    return lax.dot_general(a.astype(BF16), b.astype(BF16), TN_DIMS, preferred_element_type=F32)


def _dot_split(m, x):
    hi = x.astype(BF16)
    lo = (x - hi.astype(F32)).astype(BF16)
    return (jnp.dot(m, hi, preferred_element_type=F32) + jnp.dot(m, lo, preferred_element_type=F32))


def _rmsnorm(x, g):
    return x * lax.rsqrt(jnp.mean(x * x, axis=-1, keepdims=True) + RMS_EPS) * g


def _inproj_kernel(h_ref, g_ref, wa_ref, wb_ref, pa_ref, pb_ref):
    u = _rmsnorm(h_ref[...], g_ref[...]).astype(BF16)
    pa_ref[...] = jnp.dot(u, wa_ref[...], preferred_element_type=F32)
    pb_ref[...] = jnp.dot(u, wb_ref[...], preferred_element_type=F32)


def _inproj(h2d, g, wa, wb):
    m = h2d.shape[0]
    const = lambda i: (0, 0)
    return pl.pallas_call(
        _inproj_kernel,
        grid=(m // ROW_TILE,),
        in_specs=[
            pl.BlockSpec((ROW_TILE, D_MODEL), lambda i: (i, 0)),
            pl.BlockSpec((1, D_MODEL), const),
            pl.BlockSpec((D_MODEL, COLS_A), const, pipeline_mode=pl.Buffered(1)),
            pl.BlockSpec((D_MODEL, COLS_B), const, pipeline_mode=pl.Buffered(1)),
        ],
        out_specs=[
            pl.BlockSpec((ROW_TILE, COLS_A), lambda i: (i, 0)),
            pl.BlockSpec((ROW_TILE, COLS_B), lambda i: (i, 0)),
        ],
        out_shape=[jax.ShapeDtypeStruct((m, COLS_A), F32), jax.ShapeDtypeStruct((m, COLS_B), F32)],
        compiler_params=pltpu.CompilerParams(dimension_semantics=("arbitrary",), vmem_limit_bytes=VMEM_LIMIT),
        name="inproj",
    )(h2d, g, wa, wb)


def _wkv_chunk_terms(at, bt, kt, rt, bh, kh, v):
    row = lax.broadcasted_iota(jnp.int32, (CHUNK, CHUNK), 0)
    col = lax.broadcasted_iota(jnp.int32, (CHUNK, CHUNK), 1)
    strict = col < row
    incl = col <= row
    a_ab = jnp.where(strict, _dot_nt(at, bt), 0.0)
    a_ak = jnp.where(strict, _dot_nt(at, kt), 0.0)
    a_rb = jnp.where(incl, _dot_nt(rt, bt), 0.0)
    a_rk = jnp.where(incl, _dot_nt(rt, kt), 0.0)
    p = a_ab
    tm = jnp.where(row == col, 1.0, 0.0) + p
    n_sq = int(math.ceil(math.log2(CHUNK))) - 1
    for _ in range(n_sq):
        p = _dot(p, p)
        tm = tm + _dot(tm, p)
    atp = _dot(tm, at)
    upre = _dot(tm, _dot(a_ak, v))
    rp = rt.astype(F32) + _dot(a_rb, atp)
    ypre = _dot(a_rb, upre) + _dot(a_rk, v)
    mct = _dot_tn(atp, bh)
    ct = _dot_tn(upre, bh) + _dot_tn(v, kh)
    return rp, ypre, mct, ct


def _rwkv_kernel(pa_ref, mu_ref, w0_ref, wdu_ref, a0_ref, wau_ref, wgu_ref, kk_ref, ka_ref, rk_ref,
                 lnw_ref, lnb_ref, ones_ref, out_ref,
                 carry_ref, state_ref, at_s, bt_s, kt_s, rt_s, bh_s, kh_s, v_s, wl_s, y_s, yfull_ref):
    tt = pa_ref.shape[0]
    nc = tt // CHUNK

    @pl.when(pl.program_id(1) == 0)
    def _():
        carry_ref[...] = jnp.zeros_like(carry_ref)
        state_ref[...] = jnp.zeros_like(state_ref)

    pa = pa_ref[...]
    row = lax.broadcasted_iota(jnp.int32, pa.shape, 0)
    prev = jnp.where(row == 0, carry_ref[...], pltpu.roll(pa, 1, 0))
    carry_ref[...] = pa[tt - 1:tt, :]
    xs = pa + (prev - pa) * mu_ref[...]

    r = xs[:, 0:D_MODEL]
    k = xs[:, D_MODEL:2 * D_MODEL]
    v = xs[:, 2 * D_MODEL:3 * D_MODEL]
    o = 3 * D_MODEL
    wd = xs[:, o:o + DECAY_LORA]
    ad = xs[:, o + DECAY_LORA:o + DECAY_LORA + AAA_LORA]
    gd = xs[:, o + DECAY_LORA + AAA_LORA:COLS_A]

    lw = -EXP_M05 * _sigmoid(w0_ref[...] + _dot(jnp.tanh(wd), wdu_ref[...]))
    lr = _sigmoid(a0_ref[...] + _dot(ad, wau_ref[...]))
    g = _dot(_sigmoid(gd), wgu_ref[...])

    ones_bd = ones_ref[...]
    kk = k * kk_ref[...]
    ssq = jnp.dot((kk * kk).astype(BF16), ones_bd, preferred_element_type=F32)
    kk = kk * lax.rsqrt(jnp.maximum(ssq, 1e-24))
    kmod = k * (1.0 + (lr - 1.0) * ka_ref[...])
    bvec = kk * lr

    ri = lax.broadcasted_iota(jnp.int32, (tt, tt), 0)
    ci = lax.broadcasted_iota(jnp.int32, (tt, tt), 1)
    same = (ri // CHUNK) == (ci // CHUNK)
    tri = jnp.where(same & (ci <= ri), 1.0, 0.0).astype(BF16)
    blk = jnp.where(same, 1.0, 0.0).astype(BF16)
    cw = _dot_split(tri, lw)
    tot = _dot_split(blk, lw)

    w_inc = jnp.exp(cw)
    w_inv = jnp.exp(-cw)
    w_prev = jnp.exp(cw - lw)
    w_end = jnp.exp(tot - cw)
    w_tot = jnp.exp(tot)

    at_f = -kk * w_prev
    bt_f = bvec * w_inv
    kt_f = kmod * w_inv
    rt_f = r * w_inc
    bh_f = bvec * w_end
    kh_f = kmod * w_end
    for h in range(N_HEADS):
        sl = slice(h * HEAD, (h + 1) * HEAD)
        at_s[h] = at_f[:, sl].astype(BF16)
        bt_s[h] = bt_f[:, sl].astype(BF16)
        kt_s[h] = kt_f[:, sl].astype(BF16)
        rt_s[h] = rt_f[:, sl].astype(BF16)
        bh_s[h] = bh_f[:, sl].astype(BF16)
        kh_s[h] = kh_f[:, sl].astype(BF16)
        v_s[h] = v[:, sl].astype(BF16)
        wl_s[h] = w_tot[:, sl]

    def head_body(h, carry):
        terms = []
        for c in range(nc):
            rows = pl.ds(c * CHUNK, CHUNK)
            terms.append(_wkv_chunk_terms(at_s[h, rows, :], bt_s[h, rows, :], kt_s[h, rows, :], rt_s[h, rows, :],
                                          bh_s[h, rows, :], kh_s[h, rows, :], v_s[h, rows, :]))
        s = state_ref[h]
        for c in range(nc):
            rp, ypre, mct, ct = terms[c]
            sb = s.astype(BF16)
            y_s[h, pl.ds(c * CHUNK, CHUNK), :] = _dot_nt(rp, sb) + ypre
            s = s * wl_s[h, pl.ds(c * CHUNK, 1), :] + _dot(sb, mct) + ct
        state_ref[h] = s
        return carry

    lax.fori_loop(0, N_HEADS, head_body, 0)

    for h in range(N_HEADS):
        yfull_ref[:, h * HEAD:(h + 1) * HEAD] = y_s[h]
    y = yfull_ref[...]

    inv_n = 1.0 / HEAD
    mean = jnp.dot(y.astype(BF16), ones_bd, preferred_element_type=F32) * inv_n
    yc = y - mean
    var = jnp.dot((yc * yc).astype(BF16), ones_bd, preferred_element_type=F32) * inv_n
    yn = yc * lax.rsqrt(var + LN_X_EPS) * lnw_ref[...] + lnb_ref[...]
    bonus = jnp.dot((r * kmod * rk_ref[...]).astype(BF16), ones_bd, preferred_element_type=F32) * v
    out_ref[...] = ((yn + bonus) * g).astype(out_ref.dtype)


def _rwkv(pa, batch, t_pad, mu, w0, wdu, a0, wau, wgu, k_k, k_a, r_k, ln_w, ln_b, ones_bd):
    tt = TIME_TILE
    nt = t_pad // tt
    const = lambda b, t: (0, 0)
    vec = pl.BlockSpec((1, D_MODEL), const)
    head_bf = pltpu.VMEM((N_HEADS, tt, HEAD), BF16)
    head_f32 = pltpu.VMEM((N_HEADS, tt, HEAD), F32)
    return pl.pallas_call(
        _rwkv_kernel,
        grid=(batch, nt),
        in_specs=[
            pl.BlockSpec((tt, COLS_A), lambda b, t: (b * nt + t, 0)),
            pl.BlockSpec((1, COLS_A), const),
            vec,
            pl.BlockSpec((DECAY_LORA, D_MODEL), const),
            vec,
            pl.BlockSpec((AAA_LORA, D_MODEL), const),
            pl.BlockSpec((GATE_LORA, D_MODEL), const),
            vec, vec, vec, vec, vec,
            pl.BlockSpec((D_MODEL, D_MODEL), const),
        ],
        out_specs=pl.BlockSpec((tt, D_MODEL), lambda b, t: (b * nt + t, 0)),
        out_shape=jax.ShapeDtypeStruct((batch * t_pad, D_MODEL), BF16),
        scratch_shapes=[
            pltpu.VMEM((1, COLS_A), F32),
            pltpu.VMEM((N_HEADS, HEAD, HEAD), F32),
            head_bf, head_bf, head_bf, head_bf, head_bf, head_bf, head_bf,
            head_f32, head_f32,
            pltpu.VMEM((tt, D_MODEL), F32),
        ],
        compiler_params=pltpu.CompilerParams(dimension_semantics=("arbitrary", "arbitrary"),
                                             vmem_limit_bytes=VMEM_LIMIT),
        name="rwkv7_mix",
    )(pa, mu, w0, wdu, a0, wau, wgu, k_k, k_a, r_k, ln_w, ln_b, ones_bd)


def _lru_kernel(pb_ref, cw_ref, cb_ref, wa_ref, ba_ref, wx_ref, bx_ref, lam_ref, out_ref, xtail_ref, hcarry_ref):
    tt = pb_ref.shape[0]

    @pl.when(pl.program_id(1) == 0)
    def _():
        xtail_ref[...] = jnp.zeros_like(xtail_ref)
        hcarry_ref[...] = jnp.zeros_like(hcarry_ref)

    xb = pb_ref[:, 0:D_MODEL]
    yb = pb_ref[:, D_MODEL:2 * D_MODEL]
    tail = xtail_ref[...]
    row8 = lax.broadcasted_iota(jnp.int32, (8, D_MODEL), 0)
    xc = cb_ref[...] + xb * cw_ref[CONV_WIDTH - 1:CONV_WIDTH, :]
    for s in range(1, CONV_WIDTH):
        sh = pltpu.roll(xb, s, 0)
        head = jnp.where(row8 < s, pltpu.roll(tail, s, 0), sh[0:8])
        sh = jnp.concatenate([head, sh[8:]], axis=0)
        xc = xc + sh * cw_ref[CONV_WIDTH - 1 - s:CONV_WIDTH - s, :]
    xtail_ref[...] = xb[tt - 8:tt]

    gate_r = _sigmoid(_dot(xc, wa_ref[...]) + ba_ref[...])
    gate_i = _sigmoid(_dot(xc, wx_ref[...]) + bx_ref[...])
    nl = -lam_ref[...]
    softplus = jnp.maximum(nl, 0.0) + _log1p(jnp.exp(-jnp.abs(nl)))
    log_a = (-LRU_C * softplus) * gate_r
    a = jnp.exp(log_a)
    hs = jnp.sqrt(jnp.tanh(-log_a) * (a * a + 1.0)) * (gate_i * xc)

    row = lax.broadcasted_iota(jnp.int32, (tt, D_MODEL), 0)
    s = 1
    while s < tt:
        m = row >= s
        hs = jnp.where(m, hs + a * pltpu.roll(hs, s, 0), hs)
        a = jnp.where(m, a * pltpu.roll(a, s, 0), a)
        s *= 2
    hs = hs + a * hcarry_ref[...]
    hcarry_ref[...] = hs[tt - 1:tt]

    gelu = 0.5 * yb * (1.0 + jnp.tanh(SQRT_2_OVER_PI * (yb + 0.044715 * (yb * yb * yb))))
    out_ref[...] = (hs * gelu).astype(out_ref.dtype)


def _lru(pb, batch, t_pad, conv_w, conv_b, wa_bd, ba, wx_bd, bx, lam):
    tt = TIME_TILE
    nt = t_pad // tt
    const = lambda b, t: (0, 0)
    vec = pl.BlockSpec((1, D_MODEL), const)
    mat = pl.BlockSpec((D_MODEL, D_MODEL), const)
    return pl.pallas_call(
        _lru_kernel,
        grid=(batch, nt),
        in_specs=[
            pl.BlockSpec((tt, 2 * D_MODEL), lambda b, t: (b * nt + t, 0)),
            pl.BlockSpec((CONV_WIDTH, D_MODEL), const),
            vec, mat, vec, mat, vec, vec,
        ],
        out_specs=pl.BlockSpec((tt, D_MODEL), lambda b, t: (b * nt + t, 0)),
        out_shape=jax.ShapeDtypeStruct((batch * t_pad, D_MODEL), BF16),
        scratch_shapes=[pltpu.VMEM((8, D_MODEL), F32), pltpu.VMEM((1, D_MODEL), F32)],
        compiler_params=pltpu.CompilerParams(dimension_semantics=("arbitrary", "arbitrary"),
                                             vmem_limit_bytes=VMEM_LIMIT),
        name="rglru",
    )(pb, conv_w, conv_b, wa_bd, ba, wx_bd, bx, lam)


def _out_kernel(h_ref, yg_ref, lru_ref, gates_ref, wpa_ref, wpb_ref, wout_ref, gffn_ref, wup_ref, wdown_ref,
                gfin_ref, out_ref):
    ya = jnp.dot(yg_ref[...], wpa_ref[...], preferred_element_type=F32)
    yr = jnp.dot(lru_ref[...], wpb_ref[...], preferred_element_type=F32)
    ga = _sigmoid(gates_ref[:, 0:D_MODEL])
    gb = _sigmoid(gates_ref[:, D_MODEL:2 * D_MODEL])
    h1 = h_ref[...] + _dot(ga * ya + gb * yr, wout_ref[...])
    z = _dot(_rmsnorm(h1, gffn_ref[...]), wup_ref[...])
    z = jnp.square(jnp.maximum(z, 0.0))
    h2 = h1 + _dot(z, wdown_ref[...])
    out_ref[...] = _rmsnorm(h2, gfin_ref[...])


def _out(h2d, yg, lru, pb, wpa, wpb, wout, gffn, wup, wdown, gfin):
    m = h2d.shape[0]
    d_ff = wup.shape[1]
    const = lambda i: (0, 0)
    rows = lambda i: (i, 0)
    single = pl.Buffered(1)
    vec = pl.BlockSpec((1, D_MODEL), const)
    mat = pl.BlockSpec((D_MODEL, D_MODEL), const, pipeline_mode=single)
    return pl.pallas_call(
        _out_kernel,
        grid=(m // ROW_TILE,),
        in_specs=[
            pl.BlockSpec((ROW_TILE, D_MODEL), rows),
            pl.BlockSpec((ROW_TILE, D_MODEL), rows),
            pl.BlockSpec((ROW_TILE, D_MODEL), rows),
            pl.BlockSpec((ROW_TILE, 2 * D_MODEL), lambda i: (i, 1)),
            mat, mat, mat, vec,
            pl.BlockSpec((D_MODEL, d_ff), const, pipeline_mode=single),
            pl.BlockSpec((d_ff, D_MODEL), const, pipeline_mode=single),
            vec,
        ],
        out_specs=pl.BlockSpec((ROW_TILE, D_MODEL), rows),
        out_shape=jax.ShapeDtypeStruct((m, D_MODEL), F32),
        compiler_params=pltpu.CompilerParams(dimension_semantics=("arbitrary",), vmem_limit_bytes=VMEM_LIMIT),
        name="outproj_mlp",
    )(h2d, yg, lru, pb, wpa, wpb, wout, gffn, wup, wdown, gfin)


def _block_diag(w):
    n, b, _ = w.shape
    eye = jnp.eye(n, dtype=w.dtype)
    return (eye[:, None, :, None] * w[:, :, None, :]).reshape(n * b, n * b)


def kernel(x, meta_tokens, norm_mix_g, w_in, mu_shift, w0, w_decay_up, a0, w_aaa_up, w_gate_up, k_k, k_a, r_k,
           ln_x_w, ln_x_b, w_proj_a, conv_w, conv_b, lru_wa, lru_ba, lru_wx, lru_bx, lru_lambda, w_proj_b, w_out,
           norm_ffn_g, w_ff_up, w_ff_down, norm_final_g):
    batch, seq, d = x.shape
    assert d == D_MODEL and norm_mix_g.shape[0] == 1
    t_real = N_META + seq
    t_pad = -(-t_real // TIME_TILE) * TIME_TILE
    while (batch * t_pad) % ROW_TILE:
        t_pad += TIME_TILE

    meta = jnp.broadcast_to(meta_tokens.astype(x.dtype)[None], (batch, N_META, d))
    pad = jnp.zeros((batch, t_pad - t_real, d), x.dtype)
    h2d = jnp.concatenate([meta, x, pad], axis=1).reshape(batch * t_pad, d)

    row = lambda p: p.reshape(1, -1).astype(F32)
    w_in_bf = w_in[0].astype(BF16)
    pa, pb = _inproj(h2d, row(norm_mix_g[0]), w_in_bf[:, :COLS_A], w_in_bf[:, COLS_A:])

    ones_bd = _block_diag(jnp.ones((N_HEADS, HEAD, HEAD), BF16))
    yg = _rwkv(pa, batch, t_pad, row(mu_shift[0]), row(w0[0]), w_decay_up[0].astype(BF16), row(a0[0]),
               w_aaa_up[0].astype(BF16), w_gate_up[0].astype(BF16), row(k_k[0]), row(k_a[0]), row(r_k[0]),
               row(ln_x_w[0]), row(ln_x_b[0]), ones_bd)

    lru = _lru(pb, batch, t_pad, conv_w[0].astype(F32), row(conv_b[0]), _block_diag(lru_wa[0]).astype(BF16),
               row(lru_ba[0]), _block_diag(lru_wx[0]).astype(BF16), row(lru_bx[0]), row(lru_lambda[0]))

    out = _out(h2d, yg, lru, pb, w_proj_a[0].astype(BF16), w_proj_b[0].astype(BF16), w_out[0].astype(BF16),
               row(norm_ffn_g[0]), w_ff_up[0].astype(BF16), w_ff_down[0].astype(BF16), row(norm_final_g))
    return out.reshape(batch, t_pad, d)[:, N_META:t_real]
```

```python
import functools
import math

import jax
import jax.numpy as jnp
from jax import lax
from jax.experimental import pallas as pl
from jax.experimental.pallas import tpu as pltpu

F32 = jnp.float32
BF16 = jnp.bfloat16

D_MODEL = 1024
N_META = 16
HEAD = 64
N_HEADS = D_MODEL // HEAD
DECAY_LORA = 64
AAA_LORA = 64
GATE_LORA = 128
COLS_A = 3 * D_MODEL + DECAY_LORA + AAA_LORA + GATE_LORA
COLS_B = 4 * D_MODEL
LN_X_EPS = 64e-5
RMS_EPS = 1e-6
LRU_C = 8.0
CONV_WIDTH = 4
EXP_M05 = math.exp(-0.5)
SQRT_2_OVER_PI = math.sqrt(2.0 / math.pi)

CHUNK = 64
TIME_TILE = 192
HEADS_PER_GROUP = 8
ROW_TILE = 384
VMEM_LIMIT = 60 * 1024 * 1024

NT_DIMS = (((1,), (1,)), ((), ()))
TN_DIMS = (((0,), (0,)), ((), ()))


def _sigmoid(x):
    return 1.0 / (1.0 + jnp.exp(-x))


def _log1p(z):
    u = 1.0 + z
    return jnp.where(u == 1.0, z, jnp.log(u) * z / jnp.where(u == 1.0, 1.0, u - 1.0))


def _dot(a, b):
    return jnp.dot(a.astype(BF16), b.astype(BF16), preferred_element_type=F32)


def _dot_nt(a, b):
    return lax.dot_general(a.astype(BF16), b.astype(BF16), NT_DIMS, preferred_element_type=F32)


def _dot_tn(a, b):
    return lax.dot_general(a.astype(BF16), b.astype(BF16), TN_DIMS, preferred_element_type=F32)


def _dot_split(m, x):
    hi = x.astype(BF16)
    lo = (x - hi.astype(F32)).astype(BF16)
    return (jnp.dot(m, hi, preferred_element_type=F32) + jnp.dot(m, lo, preferred_element_type=F32))


def _rmsnorm(x, g):
    return x * lax.rsqrt(jnp.mean(x * x, axis=-1, keepdims=True) + RMS_EPS) * g


def _inproj_kernel(h_ref, g_ref, wa_ref, wb_ref, pa_ref, pb_ref):
    u = _rmsnorm(h_ref[...], g_ref[...]).astype(BF16)
    pa_ref[...] = jnp.dot(u, wa_ref[...], preferred_element_type=F32)
    pb_ref[...] = jnp.dot(u, wb_ref[...], preferred_element_type=F32)


def _inproj(h2d, g, wa, wb):
    m = h2d.shape[0]
    const = lambda i: (0, 0)
    return pl.pallas_call(
        _inproj_kernel,
        grid=(m // ROW_TILE,),
        in_specs=[
            pl.BlockSpec((ROW_TILE, D_MODEL), lambda i: (i, 0)),
            pl.BlockSpec((1, D_MODEL), const),
            pl.BlockSpec((D_MODEL, COLS_A), const, pipeline_mode=pl.Buffered(1)),
            pl.BlockSpec((D_MODEL, COLS_B), const, pipeline_mode=pl.Buffered(1)),
        ],
        out_specs=[
            pl.BlockSpec((ROW_TILE, COLS_A), lambda i: (i, 0)),
            pl.BlockSpec((ROW_TILE, COLS_B), lambda i: (i, 0)),
        ],
        out_shape=[jax.ShapeDtypeStruct((m, COLS_A), F32), jax.ShapeDtypeStruct((m, COLS_B), F32)],
        compiler_params=pltpu.CompilerParams(dimension_semantics=("arbitrary",), vmem_limit_bytes=VMEM_LIMIT),
        name="inproj",
    )(h2d, g, wa, wb)


def _wkv_terms(units):
    row = lax.broadcasted_iota(jnp.int32, (CHUNK, CHUNK), 0)
    col = lax.broadcasted_iota(jnp.int32, (CHUNK, CHUNK), 1)
    strict = col < row
    incl = col <= row
    eye = jnp.where(row == col, 1.0, 0.0)
    at, bt, kt, rt, bh, kh, v = (list(x) for x in zip(*units))
    a_ab = [jnp.where(strict, _dot_nt(a, b), 0.0) for a, b in zip(at, bt)]
    a_ak = [jnp.where(strict, _dot_nt(a, k), 0.0).astype(BF16) for a, k in zip(at, kt)]
    a_rb = [jnp.where(incl, _dot_nt(r, b), 0.0).astype(BF16) for r, b in zip(rt, bt)]
    a_rk = [jnp.where(incl, _dot_nt(r, k), 0.0).astype(BF16) for r, k in zip(rt, kt)]
    akv = [_dot(a, x).astype(BF16) for a, x in zip(a_ak, v)]
    p = a_ab
    tm = [eye + x for x in p]
    n_sq = int(math.ceil(math.log2(CHUNK))) - 1
    for _ in range(n_sq):
        pb = [x.astype(BF16) for x in p]
        p = [jnp.dot(x, x, preferred_element_type=F32) for x in pb]
        tm = [t + _dot(t, x) for t, x in zip(tm, p)]
    tmb = [t.astype(BF16) for t in tm]
    atp = [jnp.dot(t, a, preferred_element_type=F32).astype(BF16) for t, a in zip(tmb, at)]
    upre = [jnp.dot(t, x, preferred_element_type=F32).astype(BF16) for t, x in zip(tmb, akv)]
    rp = [(r.astype(F32) + jnp.dot(a, x, preferred_element_type=F32)).astype(BF16)
          for r, a, x in zip(rt, a_rb, atp)]
    mct = [_dot_tn(x, b).astype(BF16) for x, b in zip(atp, bh)]
    ypre = [jnp.dot(a, u, preferred_element_type=F32) + jnp.dot(ak, x, preferred_element_type=F32)
            for a, u, ak, x in zip(a_rb, upre, a_rk, v)]
    ct = [_dot_tn(u, b) + _dot_tn(x, k) for u, b, x, k in zip(upre, bh, v, kh)]
    return rp, ypre, mct, ct


def _rwkv_kernel(pa_ref, mu_ref, w0_ref, wdu_ref, a0_ref, wau_ref, wgu_ref, kk_ref, ka_ref, rk_ref,
                 lnw_ref, lnb_ref, ones_ref, out_ref,
                 carry_ref, state_ref, at_s, bt_s, kt_s, rt_s, bh_s, kh_s, v_s, wl_s, y_s, yfull_ref):
    tt = pa_ref.shape[0]
    nc = tt // CHUNK

    @pl.when(pl.program_id(1) == 0)
    def _():
        carry_ref[...] = jnp.zeros_like(carry_ref)
        state_ref[...] = jnp.zeros_like(state_ref)

    pa = pa_ref[...]
    row = lax.broadcasted_iota(jnp.int32, pa.shape, 0)
    prev = jnp.where(row == 0, carry_ref[...], pltpu.roll(pa, 1, 0))
    carry_ref[...] = pa[tt - 1:tt, :]
    xs = pa + (prev - pa) * mu_ref[...]

    r = xs[:, 0:D_MODEL]
    k = xs[:, D_MODEL:2 * D_MODEL]
    v = xs[:, 2 * D_MODEL:3 * D_MODEL]
    o = 3 * D_MODEL
    wd = xs[:, o:o + DECAY_LORA]
    ad = xs[:, o + DECAY_LORA:o + DECAY_LORA + AAA_LORA]
    gd = xs[:, o + DECAY_LORA + AAA_LORA:COLS_A]

    lw = -EXP_M05 * _sigmoid(w0_ref[...] + _dot(jnp.tanh(wd), wdu_ref[...]))
    lr = _sigmoid(a0_ref[...] + _dot(ad, wau_ref[...]))
    g = _dot(_sigmoid(gd), wgu_ref[...])

    ones_bd = ones_ref[...]
    kk = k * kk_ref[...]
    ssq = jnp.dot((kk * kk).astype(BF16), ones_bd, preferred_element_type=F32)
    kk = kk * lax.rsqrt(jnp.maximum(ssq, 1e-24))
    kmod = k * (1.0 + (lr - 1.0) * ka_ref[...])
    bvec = kk * lr

    ri = lax.broadcasted_iota(jnp.int32, (tt, tt), 0)
    ci = lax.broadcasted_iota(jnp.int32, (tt, tt), 1)
    same = (ri // CHUNK) == (ci // CHUNK)
    tri = jnp.where(same & (ci <= ri), 1.0, 0.0).astype(BF16)
    blk = jnp.where(same, 1.0, 0.0).astype(BF16)
    cw = _dot_split(tri, lw)
    tot = _dot_split(blk, lw)

    w_inc = jnp.exp(cw)
    w_inv = jnp.exp(-cw)
    w_prev = jnp.exp(cw - lw)
    w_end = jnp.exp(tot - cw)
    w_tot = jnp.exp(tot)

    at_f = -kk * w_prev
    bt_f = bvec * w_inv
    kt_f = kmod * w_inv
    rt_f = r * w_inc
    bh_f = bvec * w_end
    kh_f = kmod * w_end
    for h in range(N_HEADS):
        sl = slice(h * HEAD, (h + 1) * HEAD)
        at_s[h] = at_f[:, sl].astype(BF16)
        bt_s[h] = bt_f[:, sl].astype(BF16)
        kt_s[h] = kt_f[:, sl].astype(BF16)
        rt_s[h] = rt_f[:, sl].astype(BF16)
        bh_s[h] = bh_f[:, sl].astype(BF16)
        kh_s[h] = kh_f[:, sl].astype(BF16)
        v_s[h] = v[:, sl].astype(BF16)
        wl_s[h] = w_tot[:, sl]

    for h0 in range(0, N_HEADS, HEADS_PER_GROUP):
        heads = range(h0, h0 + HEADS_PER_GROUP)
        units = []
        for h in heads:
            for c in range(nc):
                rows = pl.ds(c * CHUNK, CHUNK)
                units.append(tuple(ref[h, rows, :] for ref in (at_s, bt_s, kt_s, rt_s, bh_s, kh_s, v_s)))
        rp, ypre, mct, ct = _wkv_terms(units)
        s = [state_ref[h] for h in heads]
        for c in range(nc):
            rows = pl.ds(c * CHUNK, CHUNK)
            sb = [x.astype(BF16) for x in s]
            for j, h in enumerate(heads):
                u = j * nc + c
                y_s[h, rows, :] = _dot_nt(rp[u], sb[j]) + ypre[u]
            s = [s[j] * wl_s[h, pl.ds(c * CHUNK, 1), :] + jnp.dot(sb[j], mct[j * nc + c], preferred_element_type=F32)
                 + ct[j * nc + c] for j, h in enumerate(heads)]
        for j, h in enumerate(heads):
            state_ref[h] = s[j]

    for h in range(N_HEADS):
        yfull_ref[:, h * HEAD:(h + 1) * HEAD] = y_s[h]
    y = yfull_ref[...]

    inv_n = 1.0 / HEAD
    mean = jnp.dot(y.astype(BF16), ones_bd, preferred_element_type=F32) * inv_n
    yc = y - mean
    var = jnp.dot((yc * yc).astype(BF16), ones_bd, preferred_element_type=F32) * inv_n
    yn = yc * lax.rsqrt(var + LN_X_EPS) * lnw_ref[...] + lnb_ref[...]
    bonus = jnp.dot((r * kmod * rk_ref[...]).astype(BF16), ones_bd, preferred_element_type=F32) * v
    out_ref[...] = ((yn + bonus) * g).astype(out_ref.dtype)


def _rwkv(pa, batch, t_pad, mu, w0, wdu, a0, wau, wgu, k_k, k_a, r_k, ln_w, ln_b, ones_bd):
    tt = TIME_TILE
    nt = t_pad // tt
    const = lambda b, t: (0, 0)
    vec = pl.BlockSpec((1, D_MODEL), const)
    head_bf = pltpu.VMEM((N_HEADS, tt, HEAD), BF16)
    head_f32 = pltpu.VMEM((N_HEADS, tt, HEAD), F32)
    return pl.pallas_call(
        _rwkv_kernel,
        grid=(batch, nt),
        in_specs=[
            pl.BlockSpec((tt, COLS_A), lambda b, t: (b * nt + t, 0)),
            pl.BlockSpec((1, COLS_A), const),
            vec,
            pl.BlockSpec((DECAY_LORA, D_MODEL), const),
            vec,
            pl.BlockSpec((AAA_LORA, D_MODEL), const),
            pl.BlockSpec((GATE_LORA, D_MODEL), const),
            vec, vec, vec, vec, vec,
            pl.BlockSpec((D_MODEL, D_MODEL), const),
        ],
        out_specs=pl.BlockSpec((tt, D_MODEL), lambda b, t: (b * nt + t, 0)),
        out_shape=jax.ShapeDtypeStruct((batch * t_pad, D_MODEL), BF16),
        scratch_shapes=[
            pltpu.VMEM((1, COLS_A), F32),
            pltpu.VMEM((N_HEADS, HEAD, HEAD), F32),
            head_bf, head_bf, head_bf, head_bf, head_bf, head_bf, head_bf,
            head_f32, head_f32,
            pltpu.VMEM((tt, D_MODEL), F32),
        ],
        compiler_params=pltpu.CompilerParams(dimension_semantics=("arbitrary", "arbitrary"),
                                             vmem_limit_bytes=VMEM_LIMIT),
        name="rwkv7_mix",
    )(pa, mu, w0, wdu, a0, wau, wgu, k_k, k_a, r_k, ln_w, ln_b, ones_bd)


def _lru_kernel(pb_ref, cw_ref, cb_ref, wa_ref, ba_ref, wx_ref, bx_ref, lam_ref, out_ref, xtail_ref, hcarry_ref):
    tt = pb_ref.shape[0]

    @pl.when(pl.program_id(1) == 0)
    def _():
        xtail_ref[...] = jnp.zeros_like(xtail_ref)
        hcarry_ref[...] = jnp.zeros_like(hcarry_ref)

    xb = pb_ref[:, 0:D_MODEL]
    yb = pb_ref[:, D_MODEL:2 * D_MODEL]
    tail = xtail_ref[...]
    row8 = lax.broadcasted_iota(jnp.int32, (8, D_MODEL), 0)
    xc = cb_ref[...] + xb * cw_ref[CONV_WIDTH - 1:CONV_WIDTH, :]
    for s in range(1, CONV_WIDTH):
        sh = pltpu.roll(xb, s, 0)
        head = jnp.where(row8 < s, pltpu.roll(tail, s, 0), sh[0:8])
        sh = jnp.concatenate([head, sh[8:]], axis=0)
        xc = xc + sh * cw_ref[CONV_WIDTH - 1 - s:CONV_WIDTH - s, :]
    xtail_ref[...] = xb[tt - 8:tt]

    gate_r = _sigmoid(_dot(xc, wa_ref[...]) + ba_ref[...])
    gate_i = _sigmoid(_dot(xc, wx_ref[...]) + bx_ref[...])
    nl = -lam_ref[...]
    softplus = jnp.maximum(nl, 0.0) + _log1p(jnp.exp(-jnp.abs(nl)))
    log_a = (-LRU_C * softplus) * gate_r
    a = jnp.exp(log_a)
    hs = jnp.sqrt(jnp.tanh(-log_a) * (a * a + 1.0)) * (gate_i * xc)

    row = lax.broadcasted_iota(jnp.int32, (tt, D_MODEL), 0)
    s = 1
    while s < tt:
        m = row >= s
        hs = jnp.where(m, hs + a * pltpu.roll(hs, s, 0), hs)
        a = jnp.where(m, a * pltpu.roll(a, s, 0), a)
        s *= 2
    hs = hs + a * hcarry_ref[...]
    hcarry_ref[...] = hs[tt - 1:tt]

    gelu = 0.5 * yb * (1.0 + jnp.tanh(SQRT_2_OVER_PI * (yb + 0.044715 * (yb * yb * yb))))
    out_ref[...] = (hs * gelu).astype(out_ref.dtype)


def _lru(pb, batch, t_pad, conv_w, conv_b, wa_bd, ba, wx_bd, bx, lam):
    tt = TIME_TILE
    nt = t_pad // tt
    const = lambda b, t: (0, 0)
    vec = pl.BlockSpec((1, D_MODEL), const)
    mat = pl.BlockSpec((D_MODEL, D_MODEL), const)
    return pl.pallas_call(
        _lru_kernel,
        grid=(batch, nt),
        in_specs=[
            pl.BlockSpec((tt, 2 * D_MODEL), lambda b, t: (b * nt + t, 0)),
            pl.BlockSpec((CONV_WIDTH, D_MODEL), const),
            vec, mat, vec, mat, vec, vec,
        ],
        out_specs=pl.BlockSpec((tt, D_MODEL), lambda b, t: (b * nt + t, 0)),
        out_shape=jax.ShapeDtypeStruct((batch * t_pad, D_MODEL), BF16),
        scratch_shapes=[pltpu.VMEM((8, D_MODEL), F32), pltpu.VMEM((1, D_MODEL), F32)],
        compiler_params=pltpu.CompilerParams(dimension_semantics=("arbitrary", "arbitrary"),
                                             vmem_limit_bytes=VMEM_LIMIT),
        name="rglru",
    )(pb, conv_w, conv_b, wa_bd, ba, wx_bd, bx, lam)


def _out_kernel(h_ref, yg_ref, lru_ref, gates_ref, wpa_ref, wpb_ref, wout_ref, gffn_ref, wup_ref, wdown_ref,
                gfin_ref, out_ref):
    ya = jnp.dot(yg_ref[...], wpa_ref[...], preferred_element_type=F32)
    yr = jnp.dot(lru_ref[...], wpb_ref[...], preferred_element_type=F32)
    ga = _sigmoid(gates_ref[:, 0:D_MODEL])
    gb = _sigmoid(gates_ref[:, D_MODEL:2 * D_MODEL])
    h1 = h_ref[...] + _dot(ga * ya + gb * yr, wout_ref[...])
    z = _dot(_rmsnorm(h1, gffn_ref[...]), wup_ref[...])
    z = jnp.square(jnp.maximum(z, 0.0))
    h2 = h1 + _dot(z, wdown_ref[...])
    out_ref[...] = _rmsnorm(h2, gfin_ref[...])


def _out(h2d, yg, lru, pb, wpa, wpb, wout, gffn, wup, wdown, gfin):
    m = h2d.shape[0]
    d_ff = wup.shape[1]
    const = lambda i: (0, 0)
    rows = lambda i: (i, 0)
    single = pl.Buffered(1)
    vec = pl.BlockSpec((1, D_MODEL), const)
    mat = pl.BlockSpec((D_MODEL, D_MODEL), const, pipeline_mode=single)
    return pl.pallas_call(
        _out_kernel,
        grid=(m // ROW_TILE,),
        in_specs=[
            pl.BlockSpec((ROW_TILE, D_MODEL), rows),
            pl.BlockSpec((ROW_TILE, D_MODEL), rows),
            pl.BlockSpec((ROW_TILE, D_MODEL), rows),
            pl.BlockSpec((ROW_TILE, 2 * D_MODEL), lambda i: (i, 1)),
            mat, mat, mat, vec,
            pl.BlockSpec((D_MODEL, d_ff), const, pipeline_mode=single),
            pl.BlockSpec((d_ff, D_MODEL), const, pipeline_mode=single),
            vec,
        ],
        out_specs=pl.BlockSpec((ROW_TILE, D_MODEL), rows),
        out_shape=jax.ShapeDtypeStruct((m, D_MODEL), F32),
        compiler_params=pltpu.CompilerParams(dimension_semantics=("arbitrary",), vmem_limit_bytes=VMEM_LIMIT),
        name="outproj_mlp",
    )(h2d, yg, lru, pb, wpa, wpb, wout, gffn, wup, wdown, gfin)


def _block_diag(w):
    n, b, _ = w.shape
    eye = jnp.eye(n, dtype=w.dtype)
    return (eye[:, None, :, None] * w[:, :, None, :]).reshape(n * b, n * b)


def kernel(x, meta_tokens, norm_mix_g, w_in, mu_shift, w0, w_decay_up, a0, w_aaa_up, w_gate_up, k_k, k_a, r_k,
           ln_x_w, ln_x_b, w_proj_a, conv_w, conv_b, lru_wa, lru_ba, lru_wx, lru_bx, lru_lambda, w_proj_b, w_out,
           norm_ffn_g, w_ff_up, w_ff_down, norm_final_g):
    batch, seq, d = x.shape
    assert d == D_MODEL and norm_mix_g.shape[0] == 1
    t_real = N_META + seq
    t_pad = -(-t_real // TIME_TILE) * TIME_TILE
    while (batch * t_pad) % ROW_TILE:
        t_pad += TIME_TILE

    meta = jnp.broadcast_to(meta_tokens.astype(x.dtype)[None], (batch, N_META, d))
    pad = jnp.zeros((batch, t_pad - t_real, d), x.dtype)
    h2d = jnp.concatenate([meta, x, pad], axis=1).reshape(batch * t_pad, d)

    row = lambda p: p.reshape(1, -1).astype(F32)
    w_in_bf = w_in[0].astype(BF16)
    pa, pb = _inproj(h2d, row(norm_mix_g[0]), w_in_bf[:, :COLS_A], w_in_bf[:, COLS_A:])

    ones_bd = _block_diag(jnp.ones((N_HEADS, HEAD, HEAD), BF16))
    yg = _rwkv(pa, batch, t_pad, row(mu_shift[0]), row(w0[0]), w_decay_up[0].astype(BF16), row(a0[0]),
               w_aaa_up[0].astype(BF16), w_gate_up[0].astype(BF16), row(k_k[0]), row(k_a[0]), row(r_k[0]),
               row(ln_x_w[0]), row(ln_x_b[0]), ones_bd)

    lru = _lru(pb, batch, t_pad, conv_w[0].astype(F32), row(conv_b[0]), _block_diag(lru_wa[0]).astype(BF16),
               row(lru_ba[0]), _block_diag(lru_wx[0]).astype(BF16), row(lru_bx[0]), row(lru_lambda[0]))

    out = _out(h2d, yg, lru, pb, w_proj_a[0].astype(BF16), w_proj_b[0].astype(BF16), w_out[0].astype(BF16),
               row(norm_ffn_g[0]), w_ff_up[0].astype(BF16), w_ff_down[0].astype(BF16), row(norm_final_g))
    return out.reshape(batch, t_pad, d)[:, N_META:t_real]
```

```python
import functools
import math

import jax
import jax.numpy as jnp
from jax import lax
from jax.experimental import pallas as pl
from jax.experimental.pallas import tpu as pltpu

F32 = jnp.float32
BF16 = jnp.bfloat16

D_MODEL = 1024
N_META = 16
HEAD = 64
N_HEADS = D_MODEL // HEAD
DECAY_LORA = 64
AAA_LORA = 64
GATE_LORA = 128
COLS_A = 3 * D_MODEL + DECAY_LORA + AAA_LORA + GATE_LORA
COLS_B = 4 * D_MODEL
LN_X_EPS = 64e-5
RMS_EPS = 1e-6
LRU_C = 8.0
CONV_WIDTH = 4
EXP_M05 = math.exp(-0.5)
SQRT_2_OVER_PI = math.sqrt(2.0 / math.pi)

MXU_TILE = 256
SUBLANES = 8
CHUNK = 64
TIME_TILE = 192
LANES = 128
ROW_TILE = 384
VMEM_LIMIT = 60 * 1024 * 1024

NT_DIMS = (((1,), (1,)), ((), ()))
TN_DIMS = (((0,), (0,)), ((), ()))


def _sigmoid(x):
    return 0.5 * jnp.tanh(0.5 * x) + 0.5


def _log1p(z):
    u = 1.0 + z
    return jnp.where(u == 1.0, z, jnp.log(u) * z / jnp.where(u == 1.0, 1.0, u - 1.0))


def _dot(a, b):
    return jnp.dot(a.astype(BF16), b.astype(BF16), preferred_element_type=F32)


def _dot_nt(a, b):
    return lax.dot_general(a.astype(BF16), b.astype(BF16), NT_DIMS, preferred_element_type=F32)


def _dot_tn(a, b):
    return lax.dot_general(a.astype(BF16), b.astype(BF16), TN_DIMS, preferred_element_type=F32)


def _dot_split(m, x):
    hi = x.astype(BF16)
    lo = (x - hi.astype(F32)).astype(BF16)
    return (jnp.dot(m, hi, preferred_element_type=F32) + jnp.dot(m, lo, preferred_element_type=F32))


def _rmsnorm(x, g):
    return x * lax.rsqrt(jnp.mean(x * x, axis=-1, keepdims=True) + RMS_EPS) * g


def _inproj_kernel(h_ref, g_ref, wa_ref, wb_ref, pa_ref, pb_ref):
    u = _rmsnorm(h_ref[...], g_ref[...]).astype(BF16)
    pa_ref[...] = jnp.dot(u, wa_ref[...], preferred_element_type=F32)
    pb_ref[...] = jnp.dot(u, wb_ref[...], preferred_element_type=F32)


def _inproj(h2d, g, wa, wb):
    m = h2d.shape[0]
    const = lambda i: (0, 0)
    return pl.pallas_call(
        _inproj_kernel,
        grid=(m // ROW_TILE,),
        in_specs=[
            pl.BlockSpec((ROW_TILE, D_MODEL), lambda i: (i, 0)),
            pl.BlockSpec((1, D_MODEL), const),
            pl.BlockSpec((D_MODEL, COLS_A), const, pipeline_mode=pl.Buffered(1)),
            pl.BlockSpec((D_MODEL, COLS_B), const, pipeline_mode=pl.Buffered(1)),
        ],
        out_specs=[
            pl.BlockSpec((ROW_TILE, COLS_A), lambda i: (i, 0)),
            pl.BlockSpec((ROW_TILE, COLS_B), lambda i: (i, 0)),
        ],
        out_shape=[jax.ShapeDtypeStruct((m, COLS_A), F32), jax.ShapeDtypeStruct((m, COLS_B), F32)],
        compiler_params=pltpu.CompilerParams(dimension_semantics=("arbitrary",), vmem_limit_bytes=VMEM_LIMIT),
        name="inproj",
    )(h2d, g, wa, wb)


def _wkv_terms(units):
    c = CHUNK
    f32dot = functools.partial(jnp.dot, preferred_element_type=F32)
    row = lax.broadcasted_iota(jnp.int32, (c, LANES), 0)
    lane = lax.broadcasted_iota(jnp.int32, (c, LANES), 1)
    low = lane < HEAD
    col = lane & (HEAD - 1)
    strict = col < row
    incl = col <= row
    eye_hi = jnp.where(lane - HEAD == row, 1.0, 0.0)
    zeros = jnp.zeros((c, LANES), BF16)
    at, at_hi, bt, kt, rt, rt_hi, v, bh_hi, kh_hi = (list(x) for x in zip(*units))

    bk = [jnp.concatenate([b, k], axis=0) for b, k in zip(bt, kt)]
    g = [jnp.where(strict, _dot_nt(a, x), 0.0) for a, x in zip(at, bk)]
    q = [jnp.where(incl, _dot_nt(r, x), 0.0).astype(BF16) for r, x in zip(rt, bk)]
    x0 = [(f32dot(y.astype(BF16), jnp.concatenate([zeros, vv], axis=0)) + a.astype(F32)).astype(BF16)
          for y, vv, a in zip(g, v, at_hi)]
    w = [jnp.where(low, y, eye_hi) for y in g]
    n_fac = int(math.ceil(math.log2(c)))
    for i in range(n_fac):
        wb = [x.astype(BF16) for x in w]
        pw = [f32dot(x[:, :HEAD], x) for x in wb]
        if i + 1 < n_fac:
            w = [jnp.where(low, y, x + y) for x, y in zip(w, pw)]
        else:
            w = [x + y for x, y in zip(w, pw)]
    tu = [f32dot(x.astype(BF16), jnp.concatenate([zeros, y], axis=0)).astype(BF16) for x, y in zip(w, x0)]
    rhs2 = [jnp.concatenate([x, vv], axis=0) for x, vv in zip(tu, v)]
    ry = [f32dot(x, y) for x, y in zip(q, rhs2)]
    rp = [jnp.where(low, 0.0, r.astype(F32) + y).astype(BF16) for r, y in zip(rt_hi, ry)]
    mc = [_dot_tn(y, jnp.concatenate([b, k], axis=0)) for y, b, k in zip(rhs2, bh_hi, kh_hi)]
    return rp, ry, mc


def _rwkv_kernel(pa_ref, mu_ref, w0_ref, wdu_ref, a0_ref, wau_ref, wgu_ref, kk_ref, ka_ref, rk_ref,
                 lnw_ref, lnb_ref, ones_ref, out_ref,
                 carry_ref, state_ref, at_s, ath_s, bt_s, kt_s, rt_s, rth_s, v_s, bh_s, kh_s, wl_s, y_s, yfull_ref):
    tt = pa_ref.shape[0]
    nc = tt // CHUNK

    @pl.when(pl.program_id(1) == 0)
    def _():
        carry_ref[...] = jnp.zeros_like(carry_ref)
        state_ref[...] = jnp.zeros_like(state_ref)

    pa = pa_ref[...]
    row = lax.broadcasted_iota(jnp.int32, pa.shape, 0)
    prev = jnp.where(row == 0, carry_ref[...], pltpu.roll(pa, 1, 0))
    carry_ref[...] = pa[tt - 1:tt, :]
    xs = pa + (prev - pa) * mu_ref[...]

    r = xs[:, 0:D_MODEL]
    k = xs[:, D_MODEL:2 * D_MODEL]
    v = xs[:, 2 * D_MODEL:3 * D_MODEL]
    o = 3 * D_MODEL
    wd = xs[:, o:o + DECAY_LORA]
    ad = xs[:, o + DECAY_LORA:o + DECAY_LORA + AAA_LORA]
    gd = xs[:, o + DECAY_LORA + AAA_LORA:COLS_A]

    lw = -EXP_M05 * _sigmoid(w0_ref[...] + _dot(jnp.tanh(wd), wdu_ref[...]))
    lr = _sigmoid(a0_ref[...] + _dot(ad, wau_ref[...]))
    g = _dot(_sigmoid(gd), wgu_ref[...])

    ones_bd = ones_ref[...]

    def head_sum(x):
        xb = x.astype(BF16)
        return jnp.concatenate(
            [jnp.dot(xb[:, j:j + MXU_TILE], ones_bd, preferred_element_type=F32)
             for j in range(0, D_MODEL, MXU_TILE)], axis=1)

    kk = k * kk_ref[...]
    kk = kk * lax.rsqrt(jnp.maximum(head_sum(kk * kk), 1e-24))
    kmod = k * (1.0 + (lr - 1.0) * ka_ref[...])
    bvec = kk * lr

    ri = lax.broadcasted_iota(jnp.int32, (tt, tt), 0)
    ci = lax.broadcasted_iota(jnp.int32, (tt, tt), 1)
    tri = jnp.where(((ri // CHUNK) == (ci // CHUNK)) & (ci <= ri), 1.0, 0.0).astype(BF16)
    cw = _dot_split(tri, lw)
    tot = jnp.concatenate([jnp.broadcast_to(cw[c0 + CHUNK - 1:c0 + CHUNK, :], (CHUNK, D_MODEL))
                           for c0 in range(0, tt, CHUNK)], axis=0)

    w_inc = jnp.exp(cw)
    w_inv = jnp.exp(-cw)
    w_prev = jnp.exp(cw - lw)
    w_end = jnp.exp(tot - cw)
    w_tot = jnp.exp(tot)

    at_f = -kk * w_prev
    bt_f = bvec * w_inv
    kt_f = kmod * w_inv
    rt_f = r * w_inc
    bh_f = bvec * w_end
    kh_f = kmod * w_end
    lane = lax.broadcasted_iota(jnp.int32, (tt, LANES), 1)
    low = lane < HEAD

    def split_pair(x_f, j):
        blk = x_f[:, j * LANES:(j + 1) * LANES]
        swapped = pltpu.roll(blk, HEAD, 1)
        lo = (jnp.where(low, blk, 0.0), jnp.where(low, swapped, 0.0))
        hi = (jnp.where(low, 0.0, swapped), jnp.where(low, 0.0, blk))
        return lo, hi

    for j in range(N_HEADS // 2):
        at_lo, at_hi = split_pair(at_f, j)
        bt_lo, _ = split_pair(bt_f, j)
        kt_lo, _ = split_pair(kt_f, j)
        rt_lo, rt_hi = split_pair(rt_f, j)
        v_lo, _ = split_pair(v, j)
        _, bh_hi = split_pair(bh_f, j)
        _, kh_hi = split_pair(kh_f, j)
        _, wl_hi = split_pair(w_tot, j)
        for p in range(2):
            h = 2 * j + p
            for ref, val in ((at_s, at_lo), (ath_s, at_hi), (bt_s, bt_lo), (kt_s, kt_lo), (rt_s, rt_lo),
                             (rth_s, rt_hi), (v_s, v_lo), (bh_s, bh_hi), (kh_s, kh_hi)):
                ref[h] = val[p].astype(BF16)
            wl_s[h] = wl_hi[p]

    heads = range(N_HEADS)
    units = []
    for h in heads:
        for c in range(nc):
            rows = pl.ds(c * CHUNK, CHUNK)
            units.append(tuple(ref[h, rows, :] for ref in (at_s, ath_s, bt_s, kt_s, rt_s, rth_s, v_s, bh_s, kh_s)))
    rp, ry, mc = _wkv_terms(units)
    s = [state_ref[h] for h in heads]
    for c in range(nc):
        rows = pl.ds(c * CHUNK, CHUNK)
        sb = [x.astype(BF16) for x in s]
        for h in heads:
            u = h * nc + c
            y_s[h, rows, :] = _dot_nt(rp[u], sb[h]) + ry[u][:, :HEAD]
        s = [s[h] * wl_s[h, pl.ds(c * CHUNK, 1), :] + jnp.dot(sb[h], mc[h * nc + c].astype(BF16),
                                                              preferred_element_type=F32)
             + mc[h * nc + c][:CHUNK] for h in heads]
    for h in heads:
        state_ref[h] = s[h]

    for h in range(N_HEADS):
        yfull_ref[:, h * HEAD:(h + 1) * HEAD] = y_s[h]
    y = yfull_ref[...]

    inv_n = 1.0 / HEAD
    mean = head_sum(y) * inv_n
    yc = y - mean
    var = head_sum(yc * yc) * inv_n
    yn = yc * lax.rsqrt(var + LN_X_EPS) * lnw_ref[...] + lnb_ref[...]
    bonus = head_sum(r * kmod * rk_ref[...]) * v
    out_ref[...] = ((yn + bonus) * g).astype(out_ref.dtype)


def _rwkv(pa, batch, t_pad, mu, w0, wdu, a0, wau, wgu, k_k, k_a, r_k, ln_w, ln_b, ones_bd):
    tt = TIME_TILE
    nt = t_pad // tt
    const = lambda b, t: (0, 0)
    vec = pl.BlockSpec((1, D_MODEL), const)
    head_bf = pltpu.VMEM((N_HEADS, tt, LANES), BF16)
    return pl.pallas_call(
        _rwkv_kernel,
        grid=(batch, nt),
        in_specs=[
            pl.BlockSpec((tt, COLS_A), lambda b, t: (b * nt + t, 0)),
            pl.BlockSpec((1, COLS_A), const),
            vec,
            pl.BlockSpec((DECAY_LORA, D_MODEL), const),
            vec,
            pl.BlockSpec((AAA_LORA, D_MODEL), const),
            pl.BlockSpec((GATE_LORA, D_MODEL), const),
            vec, vec, vec, vec, vec,
            pl.BlockSpec((MXU_TILE, MXU_TILE), const),
        ],
        out_specs=pl.BlockSpec((tt, D_MODEL), lambda b, t: (b * nt + t, 0)),
        out_shape=jax.ShapeDtypeStruct((batch * t_pad, D_MODEL), BF16),
        scratch_shapes=[
            pltpu.VMEM((1, COLS_A), F32),
            pltpu.VMEM((N_HEADS, HEAD, LANES), F32),
            head_bf, head_bf, head_bf, head_bf, head_bf, head_bf, head_bf, head_bf, head_bf,
            pltpu.VMEM((N_HEADS, tt, LANES), F32),
            pltpu.VMEM((N_HEADS, tt, HEAD), F32),
            pltpu.VMEM((tt, D_MODEL), F32),
        ],
        compiler_params=pltpu.CompilerParams(dimension_semantics=("arbitrary", "arbitrary"),
                                             vmem_limit_bytes=VMEM_LIMIT),
        name="rwkv7_mix",
    )(pa, mu, w0, wdu, a0, wau, wgu, k_k, k_a, r_k, ln_w, ln_b, ones_bd)


def _lru_kernel(pb_ref, cw_ref, cb_ref, wa_ref, ba_ref, wx_ref, bx_ref, lam_ref, out_ref, xtail_ref, hcarry_ref,
                hs_ref):
    tt = pb_ref.shape[0]

    @pl.when(pl.program_id(1) == 0)
    def _():
        xtail_ref[...] = jnp.zeros_like(xtail_ref)
        hcarry_ref[...] = jnp.zeros_like(hcarry_ref)

    xb = pb_ref[:, 0:D_MODEL]
    yb = pb_ref[:, D_MODEL:2 * D_MODEL]
    tail = xtail_ref[...]
    row8 = lax.broadcasted_iota(jnp.int32, (8, D_MODEL), 0)
    xc = cb_ref[...] + xb * cw_ref[CONV_WIDTH - 1:CONV_WIDTH, :]
    for s in range(1, CONV_WIDTH):
        sh = pltpu.roll(xb, s, 0)
        head = jnp.where(row8 < s, pltpu.roll(tail, s, 0), sh[0:8])
        sh = jnp.concatenate([head, sh[8:]], axis=0)
        xc = xc + sh * cw_ref[CONV_WIDTH - 1 - s:CONV_WIDTH - s, :]
    xtail_ref[...] = xb[tt - 8:tt]

    xcb = xc.astype(BF16)

    def block_diag_dot(w_ref):
        return jnp.concatenate(
            [jnp.dot(xcb[:, j * MXU_TILE:(j + 1) * MXU_TILE], w_ref[j], preferred_element_type=F32)
             for j in range(D_MODEL // MXU_TILE)], axis=1)

    gate_r = _sigmoid(block_diag_dot(wa_ref) + ba_ref[...])
    gate_i = _sigmoid(block_diag_dot(wx_ref) + bx_ref[...])
    nl = -lam_ref[...]
    softplus = jnp.maximum(nl, 0.0) + _log1p(jnp.exp(-jnp.abs(nl)))
    log_a = (-LRU_C * softplus) * gate_r
    a = jnp.exp(log_a)
    hs = jnp.sqrt(jnp.tanh(-log_a) * (a * a + 1.0)) * (gate_i * xc)

    sub = lax.broadcasted_iota(jnp.int32, (tt, D_MODEL), 0) & (SUBLANES - 1)
    s = 1
    while s < SUBLANES:
        m = sub >= s
        hs = jnp.where(m, hs + a * pltpu.roll(hs, s, 0), hs)
        a = jnp.where(m, a * pltpu.roll(a, s, 0), a)
        s *= 2
    hs_ref[...] = hs
    carry = hcarry_ref[...]
    for g0 in range(0, tt, SUBLANES):
        hs_ref[g0:g0 + SUBLANES, :] = hs[g0:g0 + SUBLANES] + a[g0:g0 + SUBLANES] * carry
        carry = hs_ref[g0 + SUBLANES - 1:g0 + SUBLANES, :]
    hs = hs_ref[...]
    hcarry_ref[...] = carry

    gelu = 0.5 * yb * (1.0 + jnp.tanh(SQRT_2_OVER_PI * (yb + 0.044715 * (yb * yb * yb))))
    out_ref[...] = (hs * gelu).astype(out_ref.dtype)


def _lru(pb, batch, t_pad, conv_w, conv_b, wa_bd, ba, wx_bd, bx, lam):
    tt = TIME_TILE
    nt = t_pad // tt
    const = lambda b, t: (0, 0)
    vec = pl.BlockSpec((1, D_MODEL), const)
    mat = pl.BlockSpec((D_MODEL // MXU_TILE, MXU_TILE, MXU_TILE), lambda b, t: (0, 0, 0))
    return pl.pallas_call(
        _lru_kernel,
        grid=(batch, nt),
        in_specs=[
            pl.BlockSpec((tt, 2 * D_MODEL), lambda b, t: (b * nt + t, 0)),
            pl.BlockSpec((CONV_WIDTH, D_MODEL), const),
            vec, mat, vec, mat, vec, vec,
        ],
        out_specs=pl.BlockSpec((tt, D_MODEL), lambda b, t: (b * nt + t, 0)),
        out_shape=jax.ShapeDtypeStruct((batch * t_pad, D_MODEL), BF16),
        scratch_shapes=[pltpu.VMEM((SUBLANES, D_MODEL), F32), pltpu.VMEM((1, D_MODEL), F32),
                        pltpu.VMEM((tt, D_MODEL), F32)],
        compiler_params=pltpu.CompilerParams(dimension_semantics=("arbitrary", "arbitrary"),
                                             vmem_limit_bytes=VMEM_LIMIT),
        name="rglru",
    )(pb, conv_w, conv_b, wa_bd, ba, wx_bd, bx, lam)


def _out_kernel(h_ref, yg_ref, lru_ref, gates_ref, wpa_ref, wpb_ref, wout_ref, gffn_ref, wup_ref, wdown_ref,
                gfin_ref, out_ref):
    ya = jnp.dot(yg_ref[...], wpa_ref[...], preferred_element_type=F32)
    yr = jnp.dot(lru_ref[...], wpb_ref[...], preferred_element_type=F32)
    ga = _sigmoid(gates_ref[:, 0:D_MODEL])
    gb = _sigmoid(gates_ref[:, D_MODEL:2 * D_MODEL])
    h1 = h_ref[...] + _dot(ga * ya + gb * yr, wout_ref[...])
    z = _dot(_rmsnorm(h1, gffn_ref[...]), wup_ref[...])
    z = jnp.square(jnp.maximum(z, 0.0))
    h2 = h1 + _dot(z, wdown_ref[...])
    out_ref[...] = _rmsnorm(h2, gfin_ref[...])


def _out(h2d, yg, lru, pb, wpa, wpb, wout, gffn, wup, wdown, gfin):
    m = h2d.shape[0]
    d_ff = wup.shape[1]
    const = lambda i: (0, 0)
    rows = lambda i: (i, 0)
    single = pl.Buffered(1)
    vec = pl.BlockSpec((1, D_MODEL), const)
    mat = pl.BlockSpec((D_MODEL, D_MODEL), const, pipeline_mode=single)
    return pl.pallas_call(
        _out_kernel,
        grid=(m // ROW_TILE,),
        in_specs=[
            pl.BlockSpec((ROW_TILE, D_MODEL), rows),
            pl.BlockSpec((ROW_TILE, D_MODEL), rows),
            pl.BlockSpec((ROW_TILE, D_MODEL), rows),
            pl.BlockSpec((ROW_TILE, 2 * D_MODEL), lambda i: (i, 1)),
            mat, mat, mat, vec,
            pl.BlockSpec((D_MODEL, d_ff), const, pipeline_mode=single),
            pl.BlockSpec((d_ff, D_MODEL), const, pipeline_mode=single),
            vec,
        ],
        out_specs=pl.BlockSpec((ROW_TILE, D_MODEL), rows),
        out_shape=jax.ShapeDtypeStruct((m, D_MODEL), F32),
        compiler_params=pltpu.CompilerParams(dimension_semantics=("arbitrary",), vmem_limit_bytes=VMEM_LIMIT),
        name="outproj_mlp",
    )(h2d, yg, lru, pb, wpa, wpb, wout, gffn, wup, wdown, gfin)


def _block_diag_tiles(w):
    n, b, _ = w.shape
    per = MXU_TILE // b
    w = w.reshape(n // per, per, b, b)
    eye = jnp.eye(per, dtype=w.dtype)
    return (eye[None, :, None, :, None] * w[:, :, :, None, :]).reshape(n // per, MXU_TILE, MXU_TILE)


def kernel(x, meta_tokens, norm_mix_g, w_in, mu_shift, w0, w_decay_up, a0, w_aaa_up, w_gate_up, k_k, k_a, r_k,
           ln_x_w, ln_x_b, w_proj_a, conv_w, conv_b, lru_wa, lru_ba, lru_wx, lru_bx, lru_lambda, w_proj_b, w_out,
           norm_ffn_g, w_ff_up, w_ff_down, norm_final_g):
    batch, seq, d = x.shape
    assert d == D_MODEL and norm_mix_g.shape[0] == 1
    t_real = N_META + seq
    t_pad = -(-t_real // TIME_TILE) * TIME_TILE
    while (batch * t_pad) % ROW_TILE:
        t_pad += TIME_TILE

    meta = jnp.broadcast_to(meta_tokens.astype(x.dtype)[None], (batch, N_META, d))
    pad = jnp.zeros((batch, t_pad - t_real, d), x.dtype)
    h2d = jnp.concatenate([meta, x, pad], axis=1).reshape(batch * t_pad, d)

    row = lambda p: p.reshape(1, -1).astype(F32)
    w_in_bf = w_in[0].astype(BF16)
    pa, pb = _inproj(h2d, row(norm_mix_g[0]), w_in_bf[:, :COLS_A], w_in_bf[:, COLS_A:])

    ones_bd = _block_diag_tiles(jnp.ones((MXU_TILE // HEAD, HEAD, HEAD), BF16))[0]
    yg = _rwkv(pa, batch, t_pad, row(mu_shift[0]), row(w0[0]), w_decay_up[0].astype(BF16), row(a0[0]),
               w_aaa_up[0].astype(BF16), w_gate_up[0].astype(BF16), row(k_k[0]), row(k_a[0]), row(r_k[0]),
               row(ln_x_w[0]), row(ln_x_b[0]), ones_bd)

    lru = _lru(pb, batch, t_pad, conv_w[0].astype(F32), row(conv_b[0]), _block_diag_tiles(lru_wa[0]).astype(BF16),
               row(lru_ba[0]), _block_diag_tiles(lru_wx[0]).astype(BF16), row(lru_bx[0]), row(lru_lambda[0]))

    out = _out(h2d, yg, lru, pb, w_proj_a[0].astype(BF16), w_proj_b[0].astype(BF16), w_out[0].astype(BF16),
               row(norm_ffn_g[0]), w_ff_up[0].astype(BF16), w_ff_down[0].astype(BF16), row(norm_final_g))
    return out.reshape(batch, t_pad, d)[:, N_META:t_real]
```

```python
import functools
import math

import jax
import jax.numpy as jnp
from jax import lax
from jax.experimental import pallas as pl
from jax.experimental.pallas import tpu as pltpu

F32 = jnp.float32
BF16 = jnp.bfloat16

D_MODEL = 1024
N_META = 16
HEAD = 64
N_HEADS = D_MODEL // HEAD
DECAY_LORA = 64
AAA_LORA = 64
GATE_LORA = 128
COLS_A = 3 * D_MODEL + DECAY_LORA + AAA_LORA + GATE_LORA
COLS_B = 4 * D_MODEL
LN_X_EPS = 64e-5
RMS_EPS = 1e-6
LRU_C = 8.0
CONV_WIDTH = 4
EXP_M05 = math.exp(-0.5)
SQRT_2_OVER_PI = math.sqrt(2.0 / math.pi)

MXU_TILE = 256
SUBLANES = 8
CHUNK = 64
TIME_TILE = 192
LANES = 128
HEADS_PER_GROUP = 8
ROW_TILE = 384
VMEM_LIMIT = 60 * 1024 * 1024

NT_DIMS = (((1,), (1,)), ((), ()))
TN_DIMS = (((0,), (0,)), ((), ()))


def _sigmoid(x):
    return 0.5 * jnp.tanh(0.5 * x) + 0.5


def _log1p(z):
    u = 1.0 + z
    return jnp.where(u == 1.0, z, jnp.log(u) * z / jnp.where(u == 1.0, 1.0, u - 1.0))


def _dot(a, b):
    return jnp.dot(a.astype(BF16), b.astype(BF16), preferred_element_type=F32)


def _dot_nt(a, b):
    return lax.dot_general(a.astype(BF16), b.astype(BF16), NT_DIMS, preferred_element_type=F32)


def _dot_tn(a, b):
    return lax.dot_general(a.astype(BF16), b.astype(BF16), TN_DIMS, preferred_element_type=F32)


def _dot_split(m, x):
    hi = x.astype(BF16)
    lo = (x - hi.astype(F32)).astype(BF16)
    return (jnp.dot(m, hi, preferred_element_type=F32) + jnp.dot(m, lo, preferred_element_type=F32))


def _rmsnorm(x, g):
    return x * lax.rsqrt(jnp.mean(x * x, axis=-1, keepdims=True) + RMS_EPS) * g


def _causal_conv(xb, cw_ref, cb_ref, xtail_ref):
    tt = xb.shape[0]
    tail = xtail_ref[...]
    row8 = lax.broadcasted_iota(jnp.int32, (SUBLANES, D_MODEL), 0)
    xc = cb_ref[...] + xb * cw_ref[CONV_WIDTH - 1:CONV_WIDTH, :]
    for s in range(1, CONV_WIDTH):
        sh = pltpu.roll(xb, s, 0)
        head = jnp.where(row8 < s, pltpu.roll(tail, s, 0), sh[0:SUBLANES])
        sh = jnp.concatenate([head, sh[SUBLANES:]], axis=0)
        xc = xc + sh * cw_ref[CONV_WIDTH - 1 - s:CONV_WIDTH - s, :]
    xtail_ref[...] = xb[tt - SUBLANES:tt]
    return xc


def _block_diag_dot(xb16, w_ref):
    return jnp.concatenate(
        [jnp.dot(xb16[:, j * MXU_TILE:(j + 1) * MXU_TILE], w_ref[j], preferred_element_type=F32)
         for j in range(D_MODEL // MXU_TILE)], axis=1)


def _rglru_recurrence(xc, yb, pre_r, pre_i, lam_ref, hcarry_ref, hs_ref):
    tt = xc.shape[0]
    gate_r = _sigmoid(pre_r)
    gate_i = _sigmoid(pre_i)
    nl = -lam_ref[...]
    softplus = jnp.maximum(nl, 0.0) + _log1p(jnp.exp(-jnp.abs(nl)))
    log_a = (-LRU_C * softplus) * gate_r
    a = jnp.exp(log_a)
    hs = jnp.sqrt(jnp.tanh(-log_a) * (a * a + 1.0)) * (gate_i * xc)

    sub = lax.broadcasted_iota(jnp.int32, (tt, D_MODEL), 0) & (SUBLANES - 1)
    s = 1
    while s < SUBLANES:
        m = sub >= s
        hs = jnp.where(m, hs + a * pltpu.roll(hs, s, 0), hs)
        a = jnp.where(m, a * pltpu.roll(a, s, 0), a)
        s *= 2
    hs_ref[...] = hs
    carry = hcarry_ref[...]
    for g0 in range(0, tt, SUBLANES):
        hs_ref[g0:g0 + SUBLANES, :] = hs[g0:g0 + SUBLANES] + a[g0:g0 + SUBLANES] * carry
        carry = hs_ref[g0 + SUBLANES - 1:g0 + SUBLANES, :]
    hs = hs_ref[...]
    hcarry_ref[...] = carry

    gelu = 0.5 * yb * (1.0 + jnp.tanh(SQRT_2_OVER_PI * (yb + 0.044715 * (yb * yb * yb))))
    return hs * gelu


def _inproj_kernel(h_ref, g_ref, wa_ref, wlru_ref, wgate_ref, mu_ref, cw_ref, cb_ref, lwa_ref, ba_ref, lwx_ref,
                   bx_ref, lam_ref, xs_ref, lru_ref, gates_ref, pcarry_ref, xtail_ref, hcarry_ref, hs_ref, pa_ref):
    tt = h_ref.shape[0]

    @pl.when(pl.program_id(1) == 0)
    def _():
        pcarry_ref[...] = jnp.zeros_like(pcarry_ref)
        xtail_ref[...] = jnp.zeros_like(xtail_ref)
        hcarry_ref[...] = jnp.zeros_like(hcarry_ref)

    u = _rmsnorm(h_ref[...], g_ref[...]).astype(BF16)

    pl_ = jnp.dot(u, wlru_ref[...], preferred_element_type=F32)
    xc = _causal_conv(pl_[:, 0:D_MODEL], cw_ref, cb_ref, xtail_ref)
    pa_ref[:, 0:D_MODEL] = jnp.dot(u, wa_ref[:, 0:D_MODEL], preferred_element_type=F32)
    xcb = xc.astype(BF16)
    pre_r = _block_diag_dot(xcb, lwa_ref) + ba_ref[...]
    pre_i = _block_diag_dot(xcb, lwx_ref) + bx_ref[...]
    pa_ref[:, D_MODEL:COLS_A] = jnp.dot(u, wa_ref[:, D_MODEL:COLS_A], preferred_element_type=F32)
    gates_ref[...] = jnp.dot(u, wgate_ref[...], preferred_element_type=F32)
    lru_ref[...] = _rglru_recurrence(xc, pl_[:, D_MODEL:2 * D_MODEL], pre_r, pre_i, lam_ref, hcarry_ref,
                                     hs_ref).astype(lru_ref.dtype)

    pa = pa_ref[...]
    row = lax.broadcasted_iota(jnp.int32, pa.shape, 0)
    prev = jnp.where(row == 0, pcarry_ref[...], pltpu.roll(pa, 1, 0))
    pcarry_ref[...] = pa[tt - 1:tt, :]
    xs_ref[...] = pa + (prev - pa) * mu_ref[...]


def _inproj(h2d, batch, t_pad, g, wa, wlru, wgate, mu, conv_w, conv_b, lwa, ba, lwx, bx, lam):
    tt = TIME_TILE
    nt = t_pad // tt
    const = lambda b, t: (0, 0)
    rows = lambda b, t: (b * nt + t, 0)
    single = pl.Buffered(1)
    vec = pl.BlockSpec((1, D_MODEL), const)
    bd = pl.BlockSpec((D_MODEL // MXU_TILE, MXU_TILE, MXU_TILE), lambda b, t: (0, 0, 0))
    m = batch * t_pad
    return pl.pallas_call(
        _inproj_kernel,
        grid=(batch, nt),
        in_specs=[
            pl.BlockSpec((tt, D_MODEL), rows),
            vec,
            pl.BlockSpec((D_MODEL, COLS_A), const, pipeline_mode=single),
            pl.BlockSpec((D_MODEL, 2 * D_MODEL), const, pipeline_mode=single),
            pl.BlockSpec((D_MODEL, 2 * D_MODEL), const, pipeline_mode=single),
            pl.BlockSpec((1, COLS_A), const),
            pl.BlockSpec((CONV_WIDTH, D_MODEL), const),
            vec, bd, vec, bd, vec, vec,
        ],
        out_specs=[
            pl.BlockSpec((tt, COLS_A), rows),
            pl.BlockSpec((tt, D_MODEL), rows),
            pl.BlockSpec((tt, 2 * D_MODEL), rows),
        ],
        out_shape=[jax.ShapeDtypeStruct((m, COLS_A), F32), jax.ShapeDtypeStruct((m, D_MODEL), BF16),
                   jax.ShapeDtypeStruct((m, 2 * D_MODEL), F32)],
        scratch_shapes=[
            pltpu.VMEM((1, COLS_A), F32),
            pltpu.VMEM((SUBLANES, D_MODEL), F32),
            pltpu.VMEM((1, D_MODEL), F32),
            pltpu.VMEM((tt, D_MODEL), F32),
            pltpu.VMEM((tt, COLS_A), F32),
        ],
        compiler_params=pltpu.CompilerParams(dimension_semantics=("arbitrary", "arbitrary"),
                                             vmem_limit_bytes=VMEM_LIMIT),
        name="inproj_rglru",
    )(h2d, g, wa, wlru, wgate, mu, conv_w, conv_b, lwa, ba, lwx, bx, lam)


def _wkv_terms(units):
    c = CHUNK
    f32dot = functools.partial(jnp.dot, preferred_element_type=F32)
    row = lax.broadcasted_iota(jnp.int32, (c, LANES), 0)
    lane = lax.broadcasted_iota(jnp.int32, (c, LANES), 1)
    low = lane < HEAD
    col = lane & (HEAD - 1)
    strict = col < row
    incl = col <= row
    eye_hi = jnp.where(lane - HEAD == row, 1.0, 0.0)
    zeros = jnp.zeros((c, LANES), BF16)
    at, at_hi, bt, kt, rt, rt_hi, v, bh_hi, kh_hi = (list(x) for x in zip(*units))

    gq = [_dot_nt(jnp.concatenate([a, r], axis=0), jnp.concatenate([b, k], axis=0))
          for a, r, b, k in zip(at, rt, bt, kt)]
    g = [jnp.where(strict, x[:c], 0.0) for x in gq]
    q = [jnp.where(incl, x[c:], 0.0).astype(BF16) for x in gq]
    x0 = [(f32dot(y.astype(BF16), jnp.concatenate([zeros, vv], axis=0)) + a.astype(F32)).astype(BF16)
          for y, vv, a in zip(g, v, at_hi)]
    w = [jnp.where(low, y, eye_hi) for y in g]
    n_fac = int(math.ceil(math.log2(c)))
    for i in range(n_fac):
        wb = [x.astype(BF16) for x in w]
        pw = [f32dot(x[:, :HEAD], x) for x in wb]
        if i + 1 < n_fac:
            w = [jnp.where(low, y, x + y) for x, y in zip(w, pw)]
        else:
            w = [x + y for x, y in zip(w, pw)]
    tu = [f32dot(x.astype(BF16), jnp.concatenate([zeros, y], axis=0)).astype(BF16) for x, y in zip(w, x0)]
    rhs2 = [jnp.concatenate([x, vv], axis=0) for x, vv in zip(tu, v)]
    ry = [f32dot(x, y) for x, y in zip(q, rhs2)]
    rp = [jnp.where(low, 0.0, r.astype(F32) + y).astype(BF16) for r, y in zip(rt_hi, ry)]
    mc = [_dot_tn(y, jnp.concatenate([b, k], axis=0)) for y, b, k in zip(rhs2, bh_hi, kh_hi)]
    return rp, ry, mc


def _rwkv_kernel(xs_ref, w0_ref, wdu_ref, a0_ref, wau_ref, wgu_ref, kk_ref, ka_ref, rk_ref,
                 lnw_ref, lnb_ref, ones_ref, out_ref,
                 state_ref, at_s, ath_s, bt_s, kt_s, rt_s, rth_s, v_s, bh_s, kh_s, wl_s, y_s, yfull_ref):
    tt = xs_ref.shape[0]
    nc = tt // CHUNK

    @pl.when(pl.program_id(1) == 0)
    def _():
        state_ref[...] = jnp.zeros_like(state_ref)

    r = xs_ref[:, 0:D_MODEL]
    k = xs_ref[:, D_MODEL:2 * D_MODEL]
    v = xs_ref[:, 2 * D_MODEL:3 * D_MODEL]
    o = 3 * D_MODEL
    wd = xs_ref[:, o:o + DECAY_LORA]
    ad = xs_ref[:, o + DECAY_LORA:o + DECAY_LORA + AAA_LORA]
    gd = xs_ref[:, o + DECAY_LORA + AAA_LORA:COLS_A]

    lw = -EXP_M05 * _sigmoid(w0_ref[...] + _dot(jnp.tanh(wd), wdu_ref[...]))
    lr = _sigmoid(a0_ref[...] + _dot(ad, wau_ref[...]))
    g = _dot(_sigmoid(gd), wgu_ref[...])

    ones_bd = ones_ref[...]

    def head_sum(x):
        xb = x.astype(BF16)
        return jnp.concatenate(
            [jnp.dot(xb[:, j:j + MXU_TILE], ones_bd, preferred_element_type=F32)
             for j in range(0, D_MODEL, MXU_TILE)], axis=1)

    kk = k * kk_ref[...]
    kk = kk * lax.rsqrt(jnp.maximum(head_sum(kk * kk), 1e-24))
    kmod = k * (1.0 + (lr - 1.0) * ka_ref[...])
    bvec = kk * lr

    ri = lax.broadcasted_iota(jnp.int32, (tt, tt), 0)
    ci = lax.broadcasted_iota(jnp.int32, (tt, tt), 1)
    tri = jnp.where(((ri // CHUNK) == (ci // CHUNK)) & (ci <= ri), 1.0, 0.0).astype(BF16)
    cw = _dot_split(tri, lw)
    tot = jnp.concatenate([jnp.broadcast_to(cw[c0 + CHUNK - 1:c0 + CHUNK, :], (CHUNK, D_MODEL))
                           for c0 in range(0, tt, CHUNK)], axis=0)

    w_inc = jnp.exp(cw)
    w_inv = jnp.exp(-cw)
    w_prev = jnp.exp(cw - lw)
    w_end = jnp.exp(tot - cw)
    w_tot = jnp.exp(tot)

    at_f = -kk * w_prev
    bt_f = bvec * w_inv
    kt_f = kmod * w_inv
    rt_f = r * w_inc
    bh_f = bvec * w_end
    kh_f = kmod * w_end
    lane = lax.broadcasted_iota(jnp.int32, (tt, LANES), 1)
    low = lane < HEAD

    def split_pair(x_f, j):
        blk = x_f[:, j * LANES:(j + 1) * LANES]
        swapped = pltpu.roll(blk, HEAD, 1)
        lo = (jnp.where(low, blk, 0.0), jnp.where(low, swapped, 0.0))
        hi = (jnp.where(low, 0.0, swapped), jnp.where(low, 0.0, blk))
        return lo, hi

    for j in range(N_HEADS // 2):
        at_lo, at_hi = split_pair(at_f, j)
        bt_lo, _ = split_pair(bt_f, j)
        kt_lo, _ = split_pair(kt_f, j)
        rt_lo, rt_hi = split_pair(rt_f, j)
        v_lo, _ = split_pair(v, j)
        _, bh_hi = split_pair(bh_f, j)
        _, kh_hi = split_pair(kh_f, j)
        _, wl_hi = split_pair(w_tot, j)
        for p in range(2):
            h = 2 * j + p
            for ref, val in ((at_s, at_lo), (ath_s, at_hi), (bt_s, bt_lo), (kt_s, kt_lo), (rt_s, rt_lo),
                             (rth_s, rt_hi), (v_s, v_lo), (bh_s, bh_hi), (kh_s, kh_hi)):
                ref[h] = val[p].astype(BF16)
            wl_s[h] = wl_hi[p]

    for h0 in range(0, N_HEADS, HEADS_PER_GROUP):
        heads = range(h0, h0 + HEADS_PER_GROUP)
        units = []
        for h in heads:
            for c in range(nc):
                rows = pl.ds(c * CHUNK, CHUNK)
                units.append(tuple(ref[h, rows, :]
                                   for ref in (at_s, ath_s, bt_s, kt_s, rt_s, rth_s, v_s, bh_s, kh_s)))
        rp, ry, mc = _wkv_terms(units)
        s = [state_ref[h] for h in heads]
        for c in range(nc):
            rows = pl.ds(c * CHUNK, CHUNK)
            sb = [x.astype(BF16) for x in s]
            for j, h in enumerate(heads):
                u = j * nc + c
                y_s[h, rows, :] = _dot_nt(rp[u], sb[j]) + ry[u][:, :HEAD]
            s = [s[j] * wl_s[h, pl.ds(c * CHUNK, 1), :]
                 + jnp.dot(sb[j], mc[j * nc + c].astype(BF16), preferred_element_type=F32)
                 + mc[j * nc + c][:CHUNK] for j, h in enumerate(heads)]
        for j, h in enumerate(heads):
            state_ref[h] = s[j]

    for h in range(N_HEADS):
        yfull_ref[:, h * HEAD:(h + 1) * HEAD] = y_s[h]
    y = yfull_ref[...]

    inv_n = 1.0 / HEAD
    mean = head_sum(y) * inv_n
    yc = y - mean
    var = head_sum(yc * yc) * inv_n
    yn = yc * lax.rsqrt(var + LN_X_EPS) * lnw_ref[...] + lnb_ref[...]
    bonus = head_sum(r * kmod * rk_ref[...]) * v
    out_ref[...] = ((yn + bonus) * g).astype(out_ref.dtype)


def _rwkv(xs, batch, t_pad, w0, wdu, a0, wau, wgu, k_k, k_a, r_k, ln_w, ln_b, ones_bd):
    tt = TIME_TILE
    nt = t_pad // tt
    const = lambda b, t: (0, 0)
    vec = pl.BlockSpec((1, D_MODEL), const)
    head_bf = pltpu.VMEM((N_HEADS, tt, LANES), BF16)
    return pl.pallas_call(
        _rwkv_kernel,
        grid=(batch, nt),
        in_specs=[
            pl.BlockSpec((tt, COLS_A), lambda b, t: (b * nt + t, 0)),
            vec,
            pl.BlockSpec((DECAY_LORA, D_MODEL), const),
            vec,
            pl.BlockSpec((AAA_LORA, D_MODEL), const),
            pl.BlockSpec((GATE_LORA, D_MODEL), const),
            vec, vec, vec, vec, vec,
            pl.BlockSpec((MXU_TILE, MXU_TILE), const),
        ],
        out_specs=pl.BlockSpec((tt, D_MODEL), lambda b, t: (b * nt + t, 0)),
        out_shape=jax.ShapeDtypeStruct((batch * t_pad, D_MODEL), BF16),
        scratch_shapes=[
            pltpu.VMEM((N_HEADS, HEAD, LANES), F32),
            head_bf, head_bf, head_bf, head_bf, head_bf, head_bf, head_bf, head_bf, head_bf,
            pltpu.VMEM((N_HEADS, tt, LANES), F32),
            pltpu.VMEM((N_HEADS, tt, HEAD), F32),
            pltpu.VMEM((tt, D_MODEL), F32),
        ],
        compiler_params=pltpu.CompilerParams(dimension_semantics=("arbitrary", "arbitrary"),
                                             vmem_limit_bytes=VMEM_LIMIT),
        name="rwkv7_mix",
    )(xs, w0, wdu, a0, wau, wgu, k_k, k_a, r_k, ln_w, ln_b, ones_bd)


def _out_kernel(h_ref, yg_ref, lru_ref, gates_ref, wpa_ref, wpb_ref, wout_ref, gffn_ref, wup_ref, wdown_ref,
                gfin_ref, out_ref):
    ya = jnp.dot(yg_ref[...], wpa_ref[...], preferred_element_type=F32)
    yr = jnp.dot(lru_ref[...], wpb_ref[...], preferred_element_type=F32)
    ga = _sigmoid(gates_ref[:, 0:D_MODEL])
    gb = _sigmoid(gates_ref[:, D_MODEL:2 * D_MODEL])
    h1 = h_ref[...] + _dot(ga * ya + gb * yr, wout_ref[...])
    z = _dot(_rmsnorm(h1, gffn_ref[...]), wup_ref[...])
    z = jnp.square(jnp.maximum(z, 0.0))
    h2 = h1 + _dot(z, wdown_ref[...])
    out_ref[...] = _rmsnorm(h2, gfin_ref[...])


def _out(h2d, yg, lru, gates, wpa, wpb, wout, gffn, wup, wdown, gfin):
    m = h2d.shape[0]
    d_ff = wup.shape[1]
    const = lambda i: (0, 0)
    rows = lambda i: (i, 0)
    single = pl.Buffered(1)
    vec = pl.BlockSpec((1, D_MODEL), const)
    mat = pl.BlockSpec((D_MODEL, D_MODEL), const, pipeline_mode=single)
    return pl.pallas_call(
        _out_kernel,
        grid=(m // ROW_TILE,),
        in_specs=[
            pl.BlockSpec((ROW_TILE, D_MODEL), rows),
            pl.BlockSpec((ROW_TILE, D_MODEL), rows),
            pl.BlockSpec((ROW_TILE, D_MODEL), rows),
            pl.BlockSpec((ROW_TILE, 2 * D_MODEL), rows),
            mat, mat, mat, vec,
            pl.BlockSpec((D_MODEL, d_ff), const, pipeline_mode=single),
            pl.BlockSpec((d_ff, D_MODEL), const, pipeline_mode=single),
            vec,
        ],
        out_specs=pl.BlockSpec((ROW_TILE, D_MODEL), rows),
        out_shape=jax.ShapeDtypeStruct((m, D_MODEL), F32),
        compiler_params=pltpu.CompilerParams(dimension_semantics=("arbitrary",), vmem_limit_bytes=VMEM_LIMIT),
        name="outproj_mlp",
    )(h2d, yg, lru, gates, wpa, wpb, wout, gffn, wup, wdown, gfin)


def _block_diag_tiles(w):
    n, b, _ = w.shape
    per = MXU_TILE // b
    w = w.reshape(n // per, per, b, b)
    eye = jnp.eye(per, dtype=w.dtype)
    return (eye[None, :, None, :, None] * w[:, :, :, None, :]).reshape(n // per, MXU_TILE, MXU_TILE)


def kernel(x, meta_tokens, norm_mix_g, w_in, mu_shift, w0, w_decay_up, a0, w_aaa_up, w_gate_up, k_k, k_a, r_k,
           ln_x_w, ln_x_b, w_proj_a, conv_w, conv_b, lru_wa, lru_ba, lru_wx, lru_bx, lru_lambda, w_proj_b, w_out,
           norm_ffn_g, w_ff_up, w_ff_down, norm_final_g):
    batch, seq, d = x.shape
    assert d == D_MODEL and norm_mix_g.shape[0] == 1
    t_real = N_META + seq
    t_pad = -(-t_real // TIME_TILE) * TIME_TILE
    while (batch * t_pad) % ROW_TILE:
        t_pad += TIME_TILE

    meta = jnp.broadcast_to(meta_tokens.astype(x.dtype)[None], (batch, N_META, d))
    pad = jnp.zeros((batch, t_pad - t_real, d), x.dtype)
    h2d = jnp.concatenate([meta, x, pad], axis=1).reshape(batch * t_pad, d)

    row = lambda p: p.reshape(1, -1).astype(F32)
    w_in_bf = w_in[0].astype(BF16)
    c_lru, c_gate = COLS_A, COLS_A + 2 * D_MODEL
    xs, lru, gates = _inproj(
        h2d, batch, t_pad, row(norm_mix_g[0]), w_in_bf[:, :c_lru], w_in_bf[:, c_lru:c_gate], w_in_bf[:, c_gate:],
        row(mu_shift[0]), conv_w[0].astype(F32), row(conv_b[0]), _block_diag_tiles(lru_wa[0]).astype(BF16),
        row(lru_ba[0]), _block_diag_tiles(lru_wx[0]).astype(BF16), row(lru_bx[0]), row(lru_lambda[0]))

    ones_bd = _block_diag_tiles(jnp.ones((MXU_TILE // HEAD, HEAD, HEAD), BF16))[0]
    yg = _rwkv(xs, batch, t_pad, row(w0[0]), w_decay_up[0].astype(BF16), row(a0[0]),
               w_aaa_up[0].astype(BF16), w_gate_up[0].astype(BF16), row(k_k[0]), row(k_a[0]), row(r_k[0]),
               row(ln_x_w[0]), row(ln_x_b[0]), ones_bd)

    out = _out(h2d, yg, lru, gates, w_proj_a[0].astype(BF16), w_proj_b[0].astype(BF16), w_out[0].astype(BF16),
               row(norm_ffn_g[0]), w_ff_up[0].astype(BF16), w_ff_down[0].astype(BF16), row(norm_final_g))
    return out.reshape(batch, t_pad, d)[:, N_META:t_real]
```

```python
import functools
import math

import jax
import jax.numpy as jnp
from jax import lax
from jax.experimental import pallas as pl
from jax.experimental.pallas import tpu as pltpu

F32 = jnp.float32
BF16 = jnp.bfloat16

D_MODEL = 1024
N_META = 16
HEAD = 64
N_HEADS = D_MODEL // HEAD
DECAY_LORA = 64
AAA_LORA = 64
GATE_LORA = 128
COLS_A = 3 * D_MODEL + DECAY_LORA + AAA_LORA + GATE_LORA
COLS_B = 4 * D_MODEL
LN_X_EPS = 64e-5
RMS_EPS = 1e-6
LRU_C = 8.0
CONV_WIDTH = 4
EXP_M05 = math.exp(-0.5)
SQRT_2_OVER_PI = math.sqrt(2.0 / math.pi)

MXU_TILE = 256
SUBLANES = 8
CHUNK = 64
TIME_TILE = 192
LANES = 128
HEADS_PER_GROUP = 8
OUT_TILE = 512
ROW_ALIGN = 16
VMEM_LIMIT = 60 * 1024 * 1024

NT_DIMS = (((1,), (1,)), ((), ()))
TN_DIMS = (((0,), (0,)), ((), ()))


def _sigmoid(x):
    return 0.5 * jnp.tanh(0.5 * x) + 0.5


def _log1p(z):
    u = 1.0 + z
    return jnp.where(u == 1.0, z, jnp.log(u) * z / jnp.where(u == 1.0, 1.0, u - 1.0))


def _dot(a, b):
    return jnp.dot(a.astype(BF16), b.astype(BF16), preferred_element_type=F32)


def _dot_nt(a, b):
    return lax.dot_general(a.astype(BF16), b.astype(BF16), NT_DIMS, preferred_element_type=F32)


def _dot_tn(a, b):
    return lax.dot_general(a.astype(BF16), b.astype(BF16), TN_DIMS, preferred_element_type=F32)


def _dot_split(m, x):
    hi = x.astype(BF16)
    lo = (x - hi.astype(F32)).astype(BF16)
    return (jnp.dot(m, hi, preferred_element_type=F32) + jnp.dot(m, lo, preferred_element_type=F32))


def _rmsnorm(x, g):
    return x * lax.rsqrt(jnp.mean(x * x, axis=-1, keepdims=True) + RMS_EPS) * g


def _causal_conv(xb, cw_ref, cb_ref, xtail_ref):
    tt = xb.shape[0]
    tail = xtail_ref[...]
    row8 = lax.broadcasted_iota(jnp.int32, (SUBLANES, D_MODEL), 0)
    xc = cb_ref[...] + xb * cw_ref[CONV_WIDTH - 1:CONV_WIDTH, :]
    for s in range(1, CONV_WIDTH):
        sh = pltpu.roll(xb, s, 0)
        head = jnp.where(row8 < s, pltpu.roll(tail, s, 0), sh[0:SUBLANES])
        sh = jnp.concatenate([head, sh[SUBLANES:]], axis=0)
        xc = xc + sh * cw_ref[CONV_WIDTH - 1 - s:CONV_WIDTH - s, :]
    xtail_ref[...] = xb[tt - SUBLANES:tt]
    return xc


def _block_diag_dot(xb16, w_ref):
    return jnp.concatenate(
        [jnp.dot(xb16[:, j * MXU_TILE:(j + 1) * MXU_TILE], w_ref[j], preferred_element_type=F32)
         for j in range(D_MODEL // MXU_TILE)], axis=1)


def _rglru_recurrence(xc, yb, pre_r, pre_i, lam_ref, hcarry_ref, hs_ref):
    tt = xc.shape[0]
    gate_r = _sigmoid(pre_r)
    gate_i = _sigmoid(pre_i)
    nl = -lam_ref[...]
    softplus = jnp.maximum(nl, 0.0) + _log1p(jnp.exp(-jnp.abs(nl)))
    log_a = (-LRU_C * softplus) * gate_r
    a = jnp.exp(log_a)
    hs = jnp.sqrt(jnp.tanh(-log_a) * (a * a + 1.0)) * (gate_i * xc)

    sub = lax.broadcasted_iota(jnp.int32, (tt, D_MODEL), 0) & (SUBLANES - 1)
    s = 1
    while s < SUBLANES:
        m = sub >= s
        hs = jnp.where(m, hs + a * pltpu.roll(hs, s, 0), hs)
        a = jnp.where(m, a * pltpu.roll(a, s, 0), a)
        s *= 2
    hs_ref[...] = hs
    carry = hcarry_ref[...]
    for g0 in range(0, tt, SUBLANES):
        hs_ref[g0:g0 + SUBLANES, :] = hs[g0:g0 + SUBLANES] + a[g0:g0 + SUBLANES] * carry
        carry = hs_ref[g0 + SUBLANES - 1:g0 + SUBLANES, :]
    hs = hs_ref[...]
    hcarry_ref[...] = carry

    gelu = 0.5 * yb * (1.0 + jnp.tanh(SQRT_2_OVER_PI * (yb + 0.044715 * (yb * yb * yb))))
    return hs * gelu


def _inproj_kernel(tail_rows, x_ref, meta_ref, g_ref, wa_ref, wlru_ref, wgate_ref, mu_ref, cw_ref, cb_ref, lwa_ref,
                   ba_ref, lwx_ref, bx_ref, lam_ref, xs_ref, lru_ref, gates_ref, pcarry_ref, xtail_ref, hcarry_ref,
                   hs_ref, pa_ref):
    tt = x_ref.shape[1]

    @pl.when(pl.program_id(1) == 0)
    def _():
        pcarry_ref[...] = jnp.zeros_like(pcarry_ref)
        xtail_ref[...] = jnp.zeros_like(xtail_ref)
        hcarry_ref[...] = jnp.zeros_like(hcarry_ref)

    t = pl.program_id(1)
    blk = x_ref[0]
    first =jnp.concatenate([meta_ref[...], blk[:tt - N_META]], axis=0)
    if tail_rows < tt:
        last = jnp.concatenate([blk[tt - tail_rows:], jnp.zeros((tt - tail_rows, D_MODEL), F32)], axis=0)
    else:
        last = blk
    h = jnp.where(t == 0, first, jnp.where(t == pl.num_programs(1) - 1, last, blk))
    u = _rmsnorm(h, g_ref[...]).astype(BF16)

    pl_ = jnp.dot(u, wlru_ref[...], preferred_element_type=F32)
    xc = _causal_conv(pl_[:, 0:D_MODEL], cw_ref, cb_ref, xtail_ref)
    pa_ref[:, 0:D_MODEL] = jnp.dot(u, wa_ref[:, 0:D_MODEL], preferred_element_type=F32)
    xcb = xc.astype(BF16)
    pre_r = _block_diag_dot(xcb, lwa_ref) + ba_ref[...]
    pre_i = _block_diag_dot(xcb, lwx_ref) + bx_ref[...]
    pa_ref[:, D_MODEL:COLS_A] = jnp.dot(u, wa_ref[:, D_MODEL:COLS_A], preferred_element_type=F32)
    gates_ref[...] = jnp.dot(u, wgate_ref[...], preferred_element_type=F32)
    lru_ref[...] = _rglru_recurrence(xc, pl_[:, D_MODEL:2 * D_MODEL], pre_r, pre_i, lam_ref, hcarry_ref,
                                     hs_ref).astype(lru_ref.dtype)

    pa = pa_ref[...]
    row = lax.broadcasted_iota(jnp.int32, pa.shape, 0)
    prev = jnp.where(row == 0, pcarry_ref[...], pltpu.roll(pa, 1, 0))
    pcarry_ref[...] = pa[tt - 1:tt, :]
    xs_ref[...] = pa + (prev - pa) * mu_ref[...]


def _inproj(x, meta, t_pad, g, wa, wlru, wgate, mu, conv_w, conv_b, lwa, ba, lwx, bx, lam):
    batch, seq, _ = x.shape
    tt = TIME_TILE
    nt = t_pad // tt
    assert nt >= 2 and seq >= tt and seq % ROW_ALIGN == 0 and tt % ROW_ALIGN == 0 and N_META % ROW_ALIGN == 0
    tail_rows = N_META + seq - (nt - 1) * tt
    const = lambda b, t: (0, 0)
    rows = lambda b, t: (b * nt + t, 0)
    single = pl.Buffered(1)
    vec = pl.BlockSpec((1, D_MODEL), const)
    bd = pl.BlockSpec((D_MODEL // MXU_TILE, MXU_TILE, MXU_TILE), lambda b, t: (0, 0, 0))
    m = batch * t_pad
    return pl.pallas_call(
        functools.partial(_inproj_kernel, tail_rows),
        grid=(batch, nt),
        in_specs=[
            pl.BlockSpec((pl.Element(1), pl.Element(tt), pl.Element(D_MODEL)),
                         lambda b, t: (b, ROW_ALIGN * jnp.clip((tt // ROW_ALIGN) * t - N_META // ROW_ALIGN, 0,
                                                               (seq - tt) // ROW_ALIGN), 0)),
            pl.BlockSpec((N_META, D_MODEL), const),
            vec,
            pl.BlockSpec((D_MODEL, COLS_A), const, pipeline_mode=single),
            pl.BlockSpec((D_MODEL, 2 * D_MODEL), const, pipeline_mode=single),
            pl.BlockSpec((D_MODEL, 2 * D_MODEL), const, pipeline_mode=single),
            pl.BlockSpec((1, COLS_A), const),
            pl.BlockSpec((CONV_WIDTH, D_MODEL), const),
            vec, bd, vec, bd, vec, vec,
        ],
        out_specs=[
            pl.BlockSpec((tt, COLS_A), rows),
            pl.BlockSpec((tt, D_MODEL), rows),
            pl.BlockSpec((tt, 2 * D_MODEL), rows),
        ],
        out_shape=[jax.ShapeDtypeStruct((m, COLS_A), F32), jax.ShapeDtypeStruct((m, D_MODEL), BF16),
                   jax.ShapeDtypeStruct((m, 2 * D_MODEL), F32)],
        scratch_shapes=[
            pltpu.VMEM((1, COLS_A), F32),
            pltpu.VMEM((SUBLANES, D_MODEL), F32),
            pltpu.VMEM((1, D_MODEL), F32),
            pltpu.VMEM((tt, D_MODEL), F32),
            pltpu.VMEM((tt, COLS_A), F32),
        ],
        compiler_params=pltpu.CompilerParams(dimension_semantics=("arbitrary", "arbitrary"),
                                             vmem_limit_bytes=VMEM_LIMIT),
        name="inproj_rglru",
    )(x, meta, g, wa, wlru, wgate, mu, conv_w, conv_b, lwa, ba, lwx, bx, lam)


def _wkv_terms(units):
    c = CHUNK
    f32dot = functools.partial(jnp.dot, preferred_element_type=F32)
    row = lax.broadcasted_iota(jnp.int32, (c, LANES), 0)
    lane = lax.broadcasted_iota(jnp.int32, (c, LANES), 1)
    low = lane < HEAD
    col = lane & (HEAD - 1)
    strict = col < row
    incl = col <= row
    eye_hi = jnp.where(lane - HEAD == row, 1.0, 0.0)
    zeros = jnp.zeros((c, LANES), BF16)
    at, at_hi, bt, kt, rt, rt_hi, v, bh_hi, kh_hi = (list(x) for x in zip(*units))

    gq = [_dot_nt(jnp.concatenate([a, r], axis=0), jnp.concatenate([b, k], axis=0))
          for a, r, b, k in zip(at, rt, bt, kt)]
    g = [jnp.where(strict, x[:c], 0.0) for x in gq]
    q = [jnp.where(incl, x[c:], 0.0).astype(BF16) for x in gq]
    x0 = [(f32dot(y.astype(BF16), jnp.concatenate([zeros, vv], axis=0)) + a.astype(F32)).astype(BF16)
          for y, vv, a in zip(g, v, at_hi)]
    w = [jnp.where(low, y, eye_hi) for y in g]
    n_fac = int(math.ceil(math.log2(c)))
    for i in range(n_fac):
        wb = [x.astype(BF16) for x in w]
        pw = [f32dot(x[:, :HEAD], x) for x in wb]
        if i + 1 < n_fac:
            w = [jnp.where(low, y, x + y) for x, y in zip(w, pw)]
        else:
            w = [x + y for x, y in zip(w, pw)]
    tu = [f32dot(x.astype(BF16), jnp.concatenate([zeros, y], axis=0)).astype(BF16) for x, y in zip(w, x0)]
    rhs2 = [jnp.concatenate([x, vv], axis=0) for x, vv in zip(tu, v)]
    ry = [f32dot(x, y) for x, y in zip(q, rhs2)]
    rp = [jnp.where(low, 0.0, r.astype(F32) + y).astype(BF16) for r, y in zip(rt_hi, ry)]
    mc = [_dot_tn(y, jnp.concatenate([b, k], axis=0)) for y, b, k in zip(rhs2, bh_hi, kh_hi)]
    return rp, ry, mc


def _rwkv_kernel(xs_ref, w0_ref, wdu_ref, a0_ref, wau_ref, wgu_ref, kk_ref, ka_ref, rk_ref,
                 lnw_ref, lnb_ref, ones_ref, out_ref,
                 state_ref, at_s, ath_s, bt_s, kt_s, rt_s, rth_s, v_s, bh_s, kh_s, wl_s, y_s, yfull_ref):
    tt = xs_ref.shape[0]
    nc = tt // CHUNK

    @pl.when(pl.program_id(1) == 0)
    def _():
        state_ref[...] = jnp.zeros_like(state_ref)

    r = xs_ref[:, 0:D_MODEL]
    k = xs_ref[:, D_MODEL:2 * D_MODEL]
    v = xs_ref[:, 2 * D_MODEL:3 * D_MODEL]
    o = 3 * D_MODEL
    wd = xs_ref[:, o:o + DECAY_LORA]
    ad = xs_ref[:, o + DECAY_LORA:o + DECAY_LORA + AAA_LORA]
    gd = xs_ref[:, o + DECAY_LORA + AAA_LORA:COLS_A]

    lw = -EXP_M05 * _sigmoid(w0_ref[...] + _dot(jnp.tanh(wd), wdu_ref[...]))
    lr = _sigmoid(a0_ref[...] + _dot(ad, wau_ref[...]))
    g = _dot(_sigmoid(gd), wgu_ref[...])

    ones_bd = ones_ref[...]

    def head_sum(x):
        xb = x.astype(BF16)
        return jnp.concatenate(
            [jnp.dot(xb[:, j:j + MXU_TILE], ones_bd, preferred_element_type=F32)
             for j in range(0, D_MODEL, MXU_TILE)], axis=1)

    kk = k * kk_ref[...]
    kk = kk * lax.rsqrt(jnp.maximum(head_sum(kk * kk), 1e-24))
    kmod = k * (1.0 + (lr - 1.0) * ka_ref[...])
    bvec = kk * lr

    ri = lax.broadcasted_iota(jnp.int32, (tt, tt), 0)
    ci = lax.broadcasted_iota(jnp.int32, (tt, tt), 1)
    tri = jnp.where(((ri // CHUNK) == (ci // CHUNK)) & (ci <= ri), 1.0, 0.0).astype(BF16)
    cw = _dot_split(tri, lw)
    tot = jnp.concatenate([jnp.broadcast_to(cw[c0 + CHUNK - 1:c0 + CHUNK, :], (CHUNK, D_MODEL))
                           for c0 in range(0, tt, CHUNK)], axis=0)

    w_inc = jnp.exp(cw)
    w_inv = jnp.exp(-cw)
    w_prev = jnp.exp(cw - lw)
    w_end = jnp.exp(tot - cw)
    w_tot = jnp.exp(tot)

    at_f = -kk * w_prev
    bt_f = bvec * w_inv
    kt_f = kmod * w_inv
    rt_f = r * w_inc
    bh_f = bvec * w_end
    kh_f = kmod * w_end
    lane = lax.broadcasted_iota(jnp.int32, (tt, LANES), 1)
    low = lane < HEAD

    def split_pair(x_f, j):
        blk = x_f[:, j * LANES:(j + 1) * LANES]
        swapped = pltpu.roll(blk, HEAD, 1)
        lo = (jnp.where(low, blk, 0.0), jnp.where(low, swapped, 0.0))
        hi = (jnp.where(low, 0.0, swapped), jnp.where(low, 0.0, blk))
        return lo, hi

    for j in range(N_HEADS // 2):
        at_lo, at_hi = split_pair(at_f, j)
        bt_lo, _ = split_pair(bt_f, j)
        kt_lo, _ = split_pair(kt_f, j)
        rt_lo, rt_hi = split_pair(rt_f, j)
        v_lo, _ = split_pair(v, j)
        _, bh_hi = split_pair(bh_f, j)
        _, kh_hi = split_pair(kh_f, j)
        _, wl_hi = split_pair(w_tot, j)
        for p in range(2):
            h = 2 * j + p
            for ref, val in ((at_s, at_lo), (ath_s, at_hi), (bt_s, bt_lo), (kt_s, kt_lo), (rt_s, rt_lo),
                             (rth_s, rt_hi), (v_s, v_lo), (bh_s, bh_hi), (kh_s, kh_hi)):
                ref[h] = val[p].astype(BF16)
            wl_s[h] = wl_hi[p]

    for h0 in range(0, N_HEADS, HEADS_PER_GROUP):
        heads = range(h0, h0 + HEADS_PER_GROUP)
        units = []
        for h in heads:
            for c in range(nc):
                rows = pl.ds(c * CHUNK, CHUNK)
                units.append(tuple(ref[h, rows, :]
                                   for ref in (at_s, ath_s, bt_s, kt_s, rt_s, rth_s, v_s, bh_s, kh_s)))
        rp, ry, mc = _wkv_terms(units)
        s = [state_ref[h] for h in heads]
        for c in range(nc):
            rows = pl.ds(c * CHUNK, CHUNK)
            sb = [x.astype(BF16) for x in s]
            for j, h in enumerate(heads):
                u = j * nc + c
                y_s[h, rows, :] = _dot_nt(rp[u], sb[j]) + ry[u][:, :HEAD]
            s = [s[j] * wl_s[h, pl.ds(c * CHUNK, 1), :]
                 + jnp.dot(sb[j], mc[j * nc + c].astype(BF16), preferred_element_type=F32)
                 + mc[j * nc + c][:CHUNK] for j, h in enumerate(heads)]
        for j, h in enumerate(heads):
            state_ref[h] = s[j]

    for h in range(N_HEADS):
        yfull_ref[:, h * HEAD:(h + 1) * HEAD] = y_s[h]
    y = yfull_ref[...]

    inv_n = 1.0 / HEAD
    mean = head_sum(y) * inv_n
    yc = y - mean
    var = head_sum(yc * yc) * inv_n
    yn = yc * lax.rsqrt(var + LN_X_EPS) * lnw_ref[...] + lnb_ref[...]
    bonus = head_sum(r * kmod * rk_ref[...]) * v
    out_ref[...] = ((yn + bonus) * g).astype(out_ref.dtype)


def _rwkv(xs, batch, t_pad, w0, wdu, a0, wau, wgu, k_k, k_a, r_k, ln_w, ln_b, ones_bd):
    tt = TIME_TILE
    nt = t_pad // tt
    const = lambda b, t: (0, 0)
    vec = pl.BlockSpec((1, D_MODEL), const)
    head_bf = pltpu.VMEM((N_HEADS, tt, LANES), BF16)
    return pl.pallas_call(
        _rwkv_kernel,
        grid=(batch, nt),
        in_specs=[
            pl.BlockSpec((tt, COLS_A), lambda b, t: (b * nt + t, 0)),
            vec,
            pl.BlockSpec((DECAY_LORA, D_MODEL), const),
            vec,
            pl.BlockSpec((AAA_LORA, D_MODEL), const),
            pl.BlockSpec((GATE_LORA, D_MODEL), const),
            vec, vec, vec, vec, vec,
            pl.BlockSpec((MXU_TILE, MXU_TILE), const),
        ],
        out_specs=pl.BlockSpec((tt, D_MODEL), lambda b, t: (b * nt + t, 0)),
        out_shape=jax.ShapeDtypeStruct((batch * t_pad, D_MODEL), BF16),
        scratch_shapes=[
            pltpu.VMEM((N_HEADS, HEAD, LANES), F32),
            head_bf, head_bf, head_bf, head_bf, head_bf, head_bf, head_bf, head_bf, head_bf,
            pltpu.VMEM((N_HEADS, tt, LANES), F32),
            pltpu.VMEM((N_HEADS, tt, HEAD), F32),
            pltpu.VMEM((tt, D_MODEL), F32),
        ],
        compiler_params=pltpu.CompilerParams(dimension_semantics=("arbitrary", "arbitrary"),
                                             vmem_limit_bytes=VMEM_LIMIT),
        name="rwkv7_mix",
    )(xs, w0, wdu, a0, wau, wgu, k_k, k_a, r_k, ln_w, ln_b, ones_bd)


def _out_kernel(x_ref, yg_ref, lru_ref, gates_ref, wpa_ref, wpb_ref, wout_ref, gffn_ref, wup_ref, wdown_ref,
                gfin_ref, out_ref):
    ya = jnp.dot(yg_ref[0], wpa_ref[...], preferred_element_type=F32)
    yr = jnp.dot(lru_ref[0], wpb_ref[...], preferred_element_type=F32)
    ga = _sigmoid(gates_ref[0, :, 0:D_MODEL])
    gb = _sigmoid(gates_ref[0, :, D_MODEL:2 * D_MODEL])
    h1 = x_ref[...] + _dot(ga * ya + gb * yr, wout_ref[...])
    u = _rmsnorm(h1, gffn_ref[...]).astype(BF16)
    h2 = h1
    for c0 in range(0, wup_ref.shape[1], D_MODEL):
        z = jnp.dot(u, wup_ref[:, c0:c0 + D_MODEL], preferred_element_type=F32)
        z = jnp.square(jnp.maximum(z, 0.0))
        h2 = h2 + _dot(z, wdown_ref[c0:c0 + D_MODEL, :])
    out_ref[...] = _rmsnorm(h2, gfin_ref[...])


def _out(x, yg, lru, gates, wpa, wpb, wout, gffn, wup, wdown, gfin):
    batch, seq, _ = x.shape
    d_ff = wup.shape[1]
    tile = math.gcd(seq, OUT_TILE)
    const = lambda b, j: (0, 0)
    rows = lambda b, j: (b, j, 0)
    shifted = lambda b, j: (b, ROW_ALIGN * ((tile // ROW_ALIGN) * j + N_META // ROW_ALIGN), 0)
    single = pl.Buffered(1)
    vec = pl.BlockSpec((1, D_MODEL), const)
    mat = pl.BlockSpec((D_MODEL, D_MODEL), const, pipeline_mode=single)
    return pl.pallas_call(
        _out_kernel,
        grid=(batch, seq // tile),
        in_specs=[
            pl.BlockSpec((None, tile, D_MODEL), rows),
            pl.BlockSpec((pl.Element(1), pl.Element(tile), pl.Element(D_MODEL)), shifted),
            pl.BlockSpec((pl.Element(1), pl.Element(tile), pl.Element(D_MODEL)), shifted),
            pl.BlockSpec((pl.Element(1), pl.Element(tile), pl.Element(2 * D_MODEL)), shifted),
            mat, mat, mat, vec,
            pl.BlockSpec((D_MODEL, d_ff), const, pipeline_mode=single),
            pl.BlockSpec((d_ff, D_MODEL), const, pipeline_mode=single),
            vec,
        ],
        out_specs=pl.BlockSpec((None, tile, D_MODEL), rows),
        out_shape=jax.ShapeDtypeStruct((batch, seq, D_MODEL), F32),
        compiler_params=pltpu.CompilerParams(dimension_semantics=("arbitrary", "arbitrary"),
                                             vmem_limit_bytes=VMEM_LIMIT),
        name="outproj_mlp",
    )(x, yg, lru, gates, wpa, wpb, wout, gffn, wup, wdown, gfin)


def _block_diag_tiles(w):
    n, b, _ = w.shape
    per = MXU_TILE // b
    w = w.reshape(n // per, per, b, b)
    eye = jnp.eye(per, dtype=w.dtype)
    return (eye[None, :, None, :, None] * w[:, :, :, None, :]).reshape(n // per, MXU_TILE, MXU_TILE)


def kernel(x, meta_tokens, norm_mix_g, w_in, mu_shift, w0, w_decay_up, a0, w_aaa_up, w_gate_up, k_k, k_a, r_k,
           ln_x_w, ln_x_b, w_proj_a, conv_w, conv_b, lru_wa, lru_ba, lru_wx, lru_bx, lru_lambda, w_proj_b, w_out,
           norm_ffn_g, w_ff_up, w_ff_down, norm_final_g):
    batch, seq, d = x.shape
    assert d == D_MODEL and norm_mix_g.shape[0] == 1
    t_real = N_META + seq
    t_pad = -(-t_real // TIME_TILE) * TIME_TILE

    row = lambda p: p.reshape(1, -1).astype(F32)
    w_in_bf = w_in[0].astype(BF16)
    c_lru, c_gate = COLS_A, COLS_A + 2 * D_MODEL
    xs, lru, gates = _inproj(
        x, meta_tokens.astype(x.dtype), t_pad, row(norm_mix_g[0]), w_in_bf[:, :c_lru], w_in_bf[:, c_lru:c_gate], w_in_bf[:, c_gate:],
        row(mu_shift[0]), conv_w[0].astype(F32), row(conv_b[0]), _block_diag_tiles(lru_wa[0]).astype(BF16),
        row(lru_ba[0]), _block_diag_tiles(lru_wx[0]).astype(BF16), row(lru_bx[0]), row(lru_lambda[0]))

    ones_bd = _block_diag_tiles(jnp.ones((MXU_TILE // HEAD, HEAD, HEAD), BF16))[0]
    yg = _rwkv(xs, batch, t_pad, row(w0[0]), w_decay_up[0].astype(BF16), row(a0[0]),
               w_aaa_up[0].astype(BF16), w_gate_up[0].astype(BF16), row(k_k[0]), row(k_a[0]), row(r_k[0]),
               row(ln_x_w[0]), row(ln_x_b[0]), ones_bd)

    seq3 = lambda a: a.reshape(batch, t_pad, a.shape[-1])
    return _out(x, seq3(yg), seq3(lru), seq3(gates), w_proj_a[0].astype(BF16), w_proj_b[0].astype(BF16),
                w_out[0].astype(BF16), row(norm_ffn_g[0]), w_ff_up[0].astype(BF16), w_ff_down[0].astype(BF16),
                row(norm_final_g))
```

```python
import functools
import math

import jax
import jax.numpy as jnp
from jax import lax
from jax.experimental import pallas as pl
from jax.experimental.pallas import tpu as pltpu

F32 = jnp.float32
BF16 = jnp.bfloat16

D_MODEL = 1024
N_META = 16
HEAD = 64
N_HEADS = D_MODEL // HEAD
DECAY_LORA = 64
AAA_LORA = 64
GATE_LORA = 128
COLS_A = 3 * D_MODEL + DECAY_LORA + AAA_LORA + GATE_LORA
COLS_B = 4 * D_MODEL
LN_X_EPS = 64e-5
RMS_EPS = 1e-6
LRU_C = 8.0
CONV_WIDTH = 4
EXP_M05 = math.exp(-0.5)
SQRT_2_OVER_PI = math.sqrt(2.0 / math.pi)

MXU_TILE = 256
SUBLANES = 8
CHUNK = 64
TIME_TILE = 192
LANES = 128
HEADS_PER_GROUP = 8
OUT_TILE = 512
ROW_ALIGN = 16
VMEM_LIMIT = 60 * 1024 * 1024

NT_DIMS = (((1,), (1,)), ((), ()))
TN_DIMS = (((0,), (0,)), ((), ()))


def _sigmoid(x):
    return 0.5 * jnp.tanh(0.5 * x) + 0.5


def _log1p(z):
    u = 1.0 + z
    return jnp.where(u == 1.0, z, jnp.log(u) * z / jnp.where(u == 1.0, 1.0, u - 1.0))


def _dot(a, b):
    return jnp.dot(a.astype(BF16), b.astype(BF16), preferred_element_type=F32)


def _dot_nt(a, b):
    return lax.dot_general(a.astype(BF16), b.astype(BF16), NT_DIMS, preferred_element_type=F32)


def _dot_tn(a, b):
    return lax.dot_general(a.astype(BF16), b.astype(BF16), TN_DIMS, preferred_element_type=F32)


def _dot_split(m, x):
    hi = x.astype(BF16)
    lo = (x - hi.astype(F32)).astype(BF16)
    return (jnp.dot(m, hi, preferred_element_type=F32) + jnp.dot(m, lo, preferred_element_type=F32))


def _rmsnorm(x, g):
    return x * lax.rsqrt(jnp.mean(x * x, axis=-1, keepdims=True) + RMS_EPS) * g


def _causal_conv(xb, cw_ref, cb_ref, xtail_ref):
    tt = xb.shape[0]
    tail = xtail_ref[...]
    row8 = lax.broadcasted_iota(jnp.int32, (SUBLANES, D_MODEL), 0)
    xc = cb_ref[...] + xb * cw_ref[CONV_WIDTH - 1:CONV_WIDTH, :]
    for s in range(1, CONV_WIDTH):
        sh = pltpu.roll(xb, s, 0)
        head = jnp.where(row8 < s, pltpu.roll(tail, s, 0), sh[0:SUBLANES])
        sh = jnp.concatenate([head, sh[SUBLANES:]], axis=0)
        xc = xc + sh * cw_ref[CONV_WIDTH - 1 - s:CONV_WIDTH - s, :]
    xtail_ref[...] = xb[tt - SUBLANES:tt]
    return xc


def _block_diag_dot(xb16, w_ref):
    return jnp.concatenate(
        [jnp.dot(xb16[:, j * MXU_TILE:(j + 1) * MXU_TILE], w_ref[j], preferred_element_type=F32)
         for j in range(D_MODEL // MXU_TILE)], axis=1)


def _rglru_recurrence(xc, yb, pre_r, pre_i, lam_ref, hcarry_ref, out_ref):
    tt = xc.shape[0]
    nl = -lam_ref[...]
    neg_c_softplus = -LRU_C * (jnp.maximum(nl, 0.0) + _log1p(jnp.exp(-jnp.abs(nl))))
    sub = lax.broadcasted_iota(jnp.int32, (ROW_ALIGN, D_MODEL), 0) & (SUBLANES - 1)
    carry = hcarry_ref[...]
    for r0 in range(0, tt, ROW_ALIGN):
        rows = slice(r0, r0 + ROW_ALIGN)
        log_a = neg_c_softplus * _sigmoid(pre_r[rows])
        a = jnp.exp(log_a)
        hs = jnp.sqrt(jnp.tanh(-log_a) * (a * a + 1.0)) * (_sigmoid(pre_i[rows]) * xc[rows])
        s = 1
        while s < SUBLANES:
            m = sub >= s
            hs = jnp.where(m, hs + a * pltpu.roll(hs, s, 0), hs)
            a = jnp.where(m, a * pltpu.roll(a, s, 0), a)
            s *= 2
        groups = []
        for g0 in range(0, ROW_ALIGN, SUBLANES):
            hg = hs[g0:g0 + SUBLANES] + a[g0:g0 + SUBLANES] * carry
            carry = hg[SUBLANES - 1:SUBLANES]
            groups.append(hg)
        y = yb[rows]
        gelu = 0.5 * y * (1.0 + jnp.tanh(SQRT_2_OVER_PI * (y + 0.044715 * (y * y * y))))
        out_ref[rows, :] = (jnp.concatenate(groups, axis=0) * gelu).astype(out_ref.dtype)
    hcarry_ref[...] = carry


def _inproj_kernel(tail_rows, x_ref, meta_ref, g_ref, wa_ref, wlru_ref, wgate_ref, mu_ref, cw_ref, cb_ref, lwa_ref,
                   ba_ref, lwx_ref, bx_ref, lam_ref, xs_ref, lru_ref, gates_ref, pcarry_ref, xtail_ref, hcarry_ref,
                   pa_ref):
    tt = x_ref.shape[1]

    @pl.when(pl.program_id(1) == 0)
    def _():
        pcarry_ref[...] = jnp.zeros_like(pcarry_ref)
        xtail_ref[...] = jnp.zeros_like(xtail_ref)
        hcarry_ref[...] = jnp.zeros_like(hcarry_ref)

    t = pl.program_id(1)
    blk = x_ref[0]
    first =jnp.concatenate([meta_ref[...], blk[:tt - N_META]], axis=0)
    if tail_rows < tt:
        last = jnp.concatenate([blk[tt - tail_rows:], jnp.zeros((tt - tail_rows, D_MODEL), F32)], axis=0)
    else:
        last = blk
    h = jnp.where(t == 0, first, jnp.where(t == pl.num_programs(1) - 1, last, blk))
    u = _rmsnorm(h, g_ref[...]).astype(BF16)

    pl_ = jnp.dot(u, wlru_ref[...], preferred_element_type=F32)
    xc = _causal_conv(pl_[:, 0:D_MODEL], cw_ref, cb_ref, xtail_ref)
    pa_ref[:, 0:D_MODEL] = jnp.dot(u, wa_ref[:, 0:D_MODEL], preferred_element_type=F32)
    xcb = xc.astype(BF16)
    pre_r = _block_diag_dot(xcb, lwa_ref) + ba_ref[...]
    pre_i = _block_diag_dot(xcb, lwx_ref) + bx_ref[...]
    pa_ref[:, D_MODEL:COLS_A] = jnp.dot(u, wa_ref[:, D_MODEL:COLS_A], preferred_element_type=F32)
    gates_ref[...] = jnp.dot(u, wgate_ref[...], preferred_element_type=F32)
    _rglru_recurrence(xc, pl_[:, D_MODEL:2 * D_MODEL], pre_r, pre_i, lam_ref, hcarry_ref, lru_ref)

    pa = pa_ref[...]
    row = lax.broadcasted_iota(jnp.int32, pa.shape, 0)
    prev = jnp.where(row == 0, pcarry_ref[...], pltpu.roll(pa, 1, 0))
    pcarry_ref[...] = pa[tt - 1:tt, :]
    xs_ref[...] = pa + (prev - pa) * mu_ref[...]


def _inproj(x, meta, t_pad, g, wa, wlru, wgate, mu, conv_w, conv_b, lwa, ba, lwx, bx, lam):
    batch, seq, _ = x.shape
    tt = TIME_TILE
    nt = t_pad // tt
    assert nt >= 2 and seq >= tt and seq % ROW_ALIGN == 0 and tt % ROW_ALIGN == 0 and N_META % ROW_ALIGN == 0
    tail_rows = N_META + seq - (nt - 1) * tt
    const = lambda b, t: (0, 0)
    rows = lambda b, t: (b * nt + t, 0)
    single = pl.Buffered(1)
    vec = pl.BlockSpec((1, D_MODEL), const)
    bd = pl.BlockSpec((D_MODEL // MXU_TILE, MXU_TILE, MXU_TILE), lambda b, t: (0, 0, 0))
    m = batch * t_pad
    return pl.pallas_call(
        functools.partial(_inproj_kernel, tail_rows),
        grid=(batch, nt),
        in_specs=[
            pl.BlockSpec((pl.Element(1), pl.Element(tt), pl.Element(D_MODEL)),
                         lambda b, t: (b, ROW_ALIGN * jnp.clip((tt // ROW_ALIGN) * t - N_META // ROW_ALIGN, 0,
                                                               (seq - tt) // ROW_ALIGN), 0)),
            pl.BlockSpec((N_META, D_MODEL), const),
            vec,
            pl.BlockSpec((D_MODEL, COLS_A), const, pipeline_mode=single),
            pl.BlockSpec((D_MODEL, 2 * D_MODEL), const, pipeline_mode=single),
            pl.BlockSpec((D_MODEL, 2 * D_MODEL), const, pipeline_mode=single),
            pl.BlockSpec((1, COLS_A), const),
            pl.BlockSpec((CONV_WIDTH, D_MODEL), const),
            vec, bd, vec, bd, vec, vec,
        ],
        out_specs=[
            pl.BlockSpec((tt, COLS_A), rows),
            pl.BlockSpec((tt, D_MODEL), rows),
            pl.BlockSpec((tt, 2 * D_MODEL), rows),
        ],
        out_shape=[jax.ShapeDtypeStruct((m, COLS_A), F32), jax.ShapeDtypeStruct((m, D_MODEL), BF16),
                   jax.ShapeDtypeStruct((m, 2 * D_MODEL), F32)],
        scratch_shapes=[
            pltpu.VMEM((1, COLS_A), F32),
            pltpu.VMEM((SUBLANES, D_MODEL), F32),
            pltpu.VMEM((1, D_MODEL), F32),
            pltpu.VMEM((tt, COLS_A), F32),
        ],
        compiler_params=pltpu.CompilerParams(dimension_semantics=("arbitrary", "arbitrary"),
                                             vmem_limit_bytes=VMEM_LIMIT),
        name="inproj_rglru",
    )(x, meta, g, wa, wlru, wgate, mu, conv_w, conv_b, lwa, ba, lwx, bx, lam)


def _wkv_terms(units):
    c = CHUNK
    f32dot = functools.partial(jnp.dot, preferred_element_type=F32)
    row = lax.broadcasted_iota(jnp.int32, (c, LANES), 0)
    lane = lax.broadcasted_iota(jnp.int32, (c, LANES), 1)
    low = lane < HEAD
    col = lane & (HEAD - 1)
    strict = col < row
    incl = col <= row
    eye_hi = jnp.where(lane - HEAD == row, 1.0, 0.0)
    zeros = jnp.zeros((c, LANES), BF16)
    at, at_hi, bt, kt, rt, rt_hi, v, bh_hi, kh_hi = (list(x) for x in zip(*units))

    gq = [_dot_nt(jnp.concatenate([a, r], axis=0), jnp.concatenate([b, k], axis=0))
          for a, r, b, k in zip(at, rt, bt, kt)]
    g = [jnp.where(strict, x[:c], 0.0) for x in gq]
    q = [jnp.where(incl, x[c:], 0.0).astype(BF16) for x in gq]
    x0 = [jnp.where(low, f32dot(y.astype(BF16), jnp.concatenate([zeros, vv], axis=0)).astype(BF16), a)
          for y, vv, a in zip(g, v, at_hi)]
    w = [jnp.where(low, y, eye_hi) for y in g]
    n_fac = int(math.ceil(math.log2(c)))
    for i in range(n_fac):
        wb = [x.astype(BF16) for x in w]
        pw = [f32dot(x[:, :HEAD], x) for x in wb]
        if i + 1 < n_fac:
            w = [jnp.where(low, y, x + y) for x, y in zip(w, pw)]
        else:
            w = [x + y for x, y in zip(w, pw)]
    tu = [f32dot(x.astype(BF16), jnp.concatenate([zeros, y], axis=0)).astype(BF16) for x, y in zip(w, x0)]
    rhs2 = [jnp.concatenate([x, vv], axis=0) for x, vv in zip(tu, v)]
    ry = [f32dot(x, y) for x, y in zip(q, rhs2)]
    rp = [jnp.where(low, zeros, r + y.astype(BF16)) for r, y in zip(rt_hi, ry)]
    mc = [_dot_tn(y, jnp.concatenate([b, k], axis=0)) for y, b, k in zip(rhs2, bh_hi, kh_hi)]
    return rp, ry, mc


def _rwkv_kernel(xs_ref, w0_ref, wdu_ref, a0_ref, wau_ref, wgu_ref, kk_ref, ka_ref, rk_ref,
                 lnw_ref, lnb_ref, ones_ref, out_ref,
                 state_ref, at_s, ath_s, bt_s, kt_s, rt_s, rth_s, v_s, bh_s, kh_s, wl_s, y_s, yfull_ref):
    tt = xs_ref.shape[0]
    nc = tt // CHUNK

    @pl.when(pl.program_id(1) == 0)
    def _():
        state_ref[...] = jnp.zeros_like(state_ref)

    r = xs_ref[:, 0:D_MODEL]
    k = xs_ref[:, D_MODEL:2 * D_MODEL]
    v = xs_ref[:, 2 * D_MODEL:3 * D_MODEL]
    o = 3 * D_MODEL
    wd = xs_ref[:, o:o + DECAY_LORA]
    ad = xs_ref[:, o + DECAY_LORA:o + DECAY_LORA + AAA_LORA]
    gd = xs_ref[:, o + DECAY_LORA + AAA_LORA:COLS_A]

    lw = -EXP_M05 * _sigmoid(w0_ref[...] + _dot(jnp.tanh(wd), wdu_ref[...]))
    lr = _sigmoid(a0_ref[...] + _dot(ad, wau_ref[...]))
    g = _dot(_sigmoid(gd), wgu_ref[...])

    ones_bd = ones_ref[...]

    def head_sum(x):
        xb = x.astype(BF16)
        return jnp.concatenate(
            [jnp.dot(xb[:, j:j + MXU_TILE], ones_bd, preferred_element_type=F32)
             for j in range(0, D_MODEL, MXU_TILE)], axis=1)

    kk = k * kk_ref[...]
    kk = kk * lax.rsqrt(jnp.maximum(head_sum(kk * kk), 1e-24))
    kmod = k * (1.0 + (lr - 1.0) * ka_ref[...])
    bvec = kk * lr

    ri = lax.broadcasted_iota(jnp.int32, (tt, tt), 0)
    ci = lax.broadcasted_iota(jnp.int32, (tt, tt), 1)
    tri = jnp.where(((ri // CHUNK) == (ci // CHUNK)) & (ci <= ri), 1.0, 0.0).astype(BF16)
    cw = _dot_split(tri, lw)
    tot = jnp.concatenate([jnp.broadcast_to(cw[c0 + CHUNK - 1:c0 + CHUNK, :], (CHUNK, D_MODEL))
                           for c0 in range(0, tt, CHUNK)], axis=0)

    w_inc = jnp.exp(cw)
    w_inv = jnp.exp(-cw)
    w_prev = jnp.exp(cw - lw)
    w_end = jnp.exp(tot - cw)
    w_tot = jnp.exp(tot)

    at_f = -kk * w_prev
    bt_f = bvec * w_inv
    kt_f = kmod * w_inv
    rt_f = r * w_inc
    bh_f = bvec * w_end
    kh_f = kmod * w_end
    lane = lax.broadcasted_iota(jnp.int32, (tt, LANES), 1)
    low = lane < HEAD
    zero = jnp.zeros((tt, LANES), BF16)

    def pair_views(x_f, j, dtype=BF16):
        blk = x_f[:, j * LANES:(j + 1) * LANES].astype(dtype)
        return blk, pltpu.roll(blk, HEAD, 1)

    def lo_masked(blk, swapped):
        return jnp.where(low, blk, zero), jnp.where(low, swapped, zero)

    def hi_masked(blk, swapped):
        return jnp.where(low, zero, swapped), jnp.where(low, zero, blk)

    for j in range(N_HEADS // 2):
        at_b, at_sw = pair_views(at_f, j)
        rt_b, rt_sw = pair_views(rt_f, j)
        wl_b, wl_sw = pair_views(w_tot, j, F32)
        stores = ((at_s, (at_b, at_sw)), (ath_s, (at_sw, at_b)),
                  (bt_s, lo_masked(*pair_views(bt_f, j))), (kt_s, lo_masked(*pair_views(kt_f, j))),
                  (rt_s, (rt_b, rt_sw)), (rth_s, (rt_sw, rt_b)),
                  (v_s, lo_masked(*pair_views(v, j))),
                  (bh_s, hi_masked(*pair_views(bh_f, j))), (kh_s, hi_masked(*pair_views(kh_f, j))),
                  (wl_s, (wl_sw, wl_b)))
        for ref, (even, odd) in stores:
            ref[2 * j] = even
            ref[2 * j + 1] = odd

    for h0 in range(0, N_HEADS, HEADS_PER_GROUP):
        heads = range(h0, h0 + HEADS_PER_GROUP)
        units = []
        for h in heads:
            for c in range(nc):
                rows = pl.ds(c * CHUNK, CHUNK)
                units.append(tuple(ref[h, rows, :]
                                   for ref in (at_s, ath_s, bt_s, kt_s, rt_s, rth_s, v_s, bh_s, kh_s)))
        rp, ry, mc = _wkv_terms(units)
        s = [state_ref[h] for h in heads]
        for c in range(nc):
            rows = pl.ds(c * CHUNK, CHUNK)
            sb = [x.astype(BF16) for x in s]
            for j, h in enumerate(heads):
                u = j * nc + c
                y_s[h, rows, :] = _dot_nt(rp[u], sb[j]) + ry[u][:, :HEAD]
            s = [s[j] * wl_s[h, pl.ds(c * CHUNK, 1), :]
                 + jnp.dot(sb[j], mc[j * nc + c].astype(BF16), preferred_element_type=F32)
                 + mc[j * nc + c][:CHUNK] for j, h in enumerate(heads)]
        for j, h in enumerate(heads):
            state_ref[h] = s[j]

    for h in range(N_HEADS):
        yfull_ref[:, h * HEAD:(h + 1) * HEAD] = y_s[h]
    y = yfull_ref[...]

    inv_n = 1.0 / HEAD
    mean = head_sum(y) * inv_n
    yc = y - mean
    var = head_sum(yc * yc) * inv_n
    yn = yc * lax.rsqrt(var + LN_X_EPS) * lnw_ref[...] + lnb_ref[...]
    bonus = head_sum(r * kmod * rk_ref[...]) * v
    out_ref[...] = ((yn + bonus) * g).astype(out_ref.dtype)


def _rwkv(xs, batch, t_pad, w0, wdu, a0, wau, wgu, k_k, k_a, r_k, ln_w, ln_b, ones_bd):
    tt = TIME_TILE
    nt = t_pad // tt
    const = lambda b, t: (0, 0)
    vec = pl.BlockSpec((1, D_MODEL), const)
    head_bf = pltpu.VMEM((N_HEADS, tt, LANES), BF16)
    return pl.pallas_call(
        _rwkv_kernel,
        grid=(batch, nt),
        in_specs=[
            pl.BlockSpec((tt, COLS_A), lambda b, t: (b * nt + t, 0)),
            vec,
            pl.BlockSpec((DECAY_LORA, D_MODEL), const),
            vec,
            pl.BlockSpec((AAA_LORA, D_MODEL), const),
            pl.BlockSpec((GATE_LORA, D_MODEL), const),
            vec, vec, vec, vec, vec,
            pl.BlockSpec((MXU_TILE, MXU_TILE), const),
        ],
        out_specs=pl.BlockSpec((tt, D_MODEL), lambda b, t: (b * nt + t, 0)),
        out_shape=jax.ShapeDtypeStruct((batch * t_pad, D_MODEL), BF16),
        scratch_shapes=[
            pltpu.VMEM((N_HEADS, HEAD, LANES), F32),
            head_bf, head_bf, head_bf, head_bf, head_bf, head_bf, head_bf, head_bf, head_bf,
            pltpu.VMEM((N_HEADS, tt, LANES), F32),
            pltpu.VMEM((N_HEADS, tt, HEAD), F32),
            pltpu.VMEM((tt, D_MODEL), F32),
        ],
        compiler_params=pltpu.CompilerParams(dimension_semantics=("arbitrary", "arbitrary"),
                                             vmem_limit_bytes=VMEM_LIMIT),
        name="rwkv7_mix",
    )(xs, w0, wdu, a0, wau, wgu, k_k, k_a, r_k, ln_w, ln_b, ones_bd)


def _out_kernel(x_ref, yg_ref, lru_ref, gates_ref, wpa_ref, wpb_ref, wout_ref, gffn_ref, wup_ref, wdown_ref,
                gfin_ref, out_ref):
    ya = jnp.dot(yg_ref[0], wpa_ref[...], preferred_element_type=F32)
    yr = jnp.dot(lru_ref[0], wpb_ref[...], preferred_element_type=F32)
    ga = _sigmoid(gates_ref[0, :, 0:D_MODEL])
    gb = _sigmoid(gates_ref[0, :, D_MODEL:2 * D_MODEL])
    h1 = x_ref[...] + _dot(ga * ya + gb * yr, wout_ref[...])
    u = _rmsnorm(h1, gffn_ref[...]).astype(BF16)
    h2 = h1
    for c0 in range(0, wup_ref.shape[1], D_MODEL):
        z = jnp.dot(u, wup_ref[:, c0:c0 + D_MODEL], preferred_element_type=F32)
        z = jnp.square(jnp.maximum(z, 0.0))
        h2 = h2 + _dot(z, wdown_ref[c0:c0 + D_MODEL, :])
    out_ref[...] = _rmsnorm(h2, gfin_ref[...])


def _out(x, yg, lru, gates, wpa, wpb, wout, gffn, wup, wdown, gfin):
    batch, seq, _ = x.shape
    d_ff = wup.shape[1]
    tile = math.gcd(seq, OUT_TILE)
    const = lambda b, j: (0, 0)
    rows = lambda b, j: (b, j, 0)
    shifted = lambda b, j: (b, ROW_ALIGN * ((tile // ROW_ALIGN) * j + N_META // ROW_ALIGN), 0)
    single = pl.Buffered(1)
    vec = pl.BlockSpec((1, D_MODEL), const)
    mat = pl.BlockSpec((D_MODEL, D_MODEL), const, pipeline_mode=single)
    return pl.pallas_call(
        _out_kernel,
        grid=(batch, seq // tile),
        in_specs=[
            pl.BlockSpec((None, tile, D_MODEL), rows),
            pl.BlockSpec((pl.Element(1), pl.Element(tile), pl.Element(D_MODEL)), shifted),
            pl.BlockSpec((pl.Element(1), pl.Element(tile), pl.Element(D_MODEL)), shifted),
            pl.BlockSpec((pl.Element(1), pl.Element(tile), pl.Element(2 * D_MODEL)), shifted),
            mat, mat, mat, vec,
            pl.BlockSpec((D_MODEL, d_ff), const, pipeline_mode=single),
            pl.BlockSpec((d_ff, D_MODEL), const, pipeline_mode=single),
            vec,
        ],
        out_specs=pl.BlockSpec((None, tile, D_MODEL), rows),
        out_shape=jax.ShapeDtypeStruct((batch, seq, D_MODEL), F32),
        compiler_params=pltpu.CompilerParams(dimension_semantics=("arbitrary", "arbitrary"),
                                             vmem_limit_bytes=VMEM_LIMIT),
        name="outproj_mlp",
    )(x, yg, lru, gates, wpa, wpb, wout, gffn, wup, wdown, gfin)


def _block_diag_tiles(w):
    n, b, _ = w.shape
    per = MXU_TILE // b
    w = w.reshape(n // per, per, b, b)
    eye = jnp.eye(per, dtype=w.dtype)
    return (eye[None, :, None, :, None] * w[:, :, :, None, :]).reshape(n // per, MXU_TILE, MXU_TILE)


def kernel(x, meta_tokens, norm_mix_g, w_in, mu_shift, w0, w_decay_up, a0, w_aaa_up, w_gate_up, k_k, k_a, r_k,
           ln_x_w, ln_x_b, w_proj_a, conv_w, conv_b, lru_wa, lru_ba, lru_wx, lru_bx, lru_lambda, w_proj_b, w_out,
           norm_ffn_g, w_ff_up, w_ff_down, norm_final_g):
    batch, seq, d = x.shape
    assert d == D_MODEL and norm_mix_g.shape[0] == 1
    t_real = N_META + seq
    t_pad = -(-t_real // TIME_TILE) * TIME_TILE

    row = lambda p: p.reshape(1, -1).astype(F32)
    w_in_bf = w_in[0].astype(BF16)
    c_lru, c_gate = COLS_A, COLS_A + 2 * D_MODEL
    xs, lru, gates = _inproj(
        x, meta_tokens.astype(x.dtype), t_pad, row(norm_mix_g[0]), w_in_bf[:, :c_lru], w_in_bf[:, c_lru:c_gate], w_in_bf[:, c_gate:],
        row(mu_shift[0]), conv_w[0].astype(F32), row(conv_b[0]), _block_diag_tiles(lru_wa[0]).astype(BF16),
        row(lru_ba[0]), _block_diag_tiles(lru_wx[0]).astype(BF16), row(lru_bx[0]), row(lru_lambda[0]))

    ones_bd = _block_diag_tiles(jnp.ones((MXU_TILE // HEAD, HEAD, HEAD), BF16))[0]
    yg = _rwkv(xs, batch, t_pad, row(w0[0]), w_decay_up[0].astype(BF16), row(a0[0]),
               w_aaa_up[0].astype(BF16), w_gate_up[0].astype(BF16), row(k_k[0]), row(k_a[0]), row(r_k[0]),
               row(ln_x_w[0]), row(ln_x_b[0]), ones_bd)

    seq3 = lambda a: a.reshape(batch, t_pad, a.shape[-1])
    return _out(x, seq3(yg), seq3(lru), seq3(gates), w_proj_a[0].astype(BF16), w_proj_b[0].astype(BF16),
                w_out[0].astype(BF16), row(norm_ffn_g[0]), w_ff_up[0].astype(BF16), w_ff_down[0].astype(BF16),
                row(norm_final_g))
```

```python
import functools
import math

import jax
import jax.numpy as jnp
from jax import lax
from jax.experimental import pallas as pl
from jax.experimental.pallas import tpu as pltpu

F32 = jnp.float32
BF16 = jnp.bfloat16

D_MODEL = 1024
N_META = 16
HEAD = 64
N_HEADS = D_MODEL // HEAD
DECAY_LORA = 64
AAA_LORA = 64
GATE_LORA = 128
COLS_A = 3 * D_MODEL + DECAY_LORA + AAA_LORA + GATE_LORA
COLS_B = 4 * D_MODEL
LN_X_EPS = 64e-5
RMS_EPS = 1e-6
LRU_C = 8.0
CONV_WIDTH = 4
EXP_M05 = math.exp(-0.5)
SQRT_2_OVER_PI = math.sqrt(2.0 / math.pi)

MXU_TILE = 256
SUBLANES = 8
CHUNK = 64
TIME_TILE = 192
LANES = 128
HEADS_PER_GROUP = 16
OUT_TILE = 512
ROW_ALIGN = 16
VMEM_LIMIT = 60 * 1024 * 1024

NT_DIMS = (((1,), (1,)), ((), ()))
TN_DIMS = (((0,), (0,)), ((), ()))


def _sigmoid(x):
    return 0.5 * jnp.tanh(0.5 * x) + 0.5


def _log1p(z):
    u = 1.0 + z
    return jnp.where(u == 1.0, z, jnp.log(u) * z / jnp.where(u == 1.0, 1.0, u - 1.0))


def _dot(a, b):
    return jnp.dot(a.astype(BF16), b.astype(BF16), preferred_element_type=F32)


def _dot_nt(a, b):
    return lax.dot_general(a.astype(BF16), b.astype(BF16), NT_DIMS, preferred_element_type=F32)


def _dot_tn(a, b):
    return lax.dot_general(a.astype(BF16), b.astype(BF16), TN_DIMS, preferred_element_type=F32)


def _dot_split(m, x):
    hi = x.astype(BF16)
    lo = (x - hi.astype(F32)).astype(BF16)
    return (jnp.dot(m, hi, preferred_element_type=F32) + jnp.dot(m, lo, preferred_element_type=F32))


def _rmsnorm(x, g):
    return x * lax.rsqrt(jnp.mean(x * x, axis=-1, keepdims=True) + RMS_EPS) * g


def _causal_conv(xb, cw_ref, cb_ref, xtail_ref):
    tt = xb.shape[0]
    tail = xtail_ref[...]
    row8 = lax.broadcasted_iota(jnp.int32, (SUBLANES, D_MODEL), 0)
    xc = cb_ref[...] + xb * cw_ref[CONV_WIDTH - 1:CONV_WIDTH, :]
    for s in range(1, CONV_WIDTH):
        sh = pltpu.roll(xb, s, 0)
        head = jnp.where(row8 < s, pltpu.roll(tail, s, 0), sh[0:SUBLANES])
        sh = jnp.concatenate([head, sh[SUBLANES:]], axis=0)
        xc = xc + sh * cw_ref[CONV_WIDTH - 1 - s:CONV_WIDTH - s, :]
    xtail_ref[...] = xb[tt - SUBLANES:tt]
    return xc


def _block_diag_dot(xb16, w_ref):
    return jnp.concatenate(
        [jnp.dot(xb16[:, j * MXU_TILE:(j + 1) * MXU_TILE], w_ref[j], preferred_element_type=F32)
         for j in range(D_MODEL // MXU_TILE)], axis=1)


def _rglru_recurrence(xc, yb, pre_r, pre_i, lam_ref, hcarry_ref, out_ref):
    tt = xc.shape[0]
    nl = -lam_ref[...]
    neg_c_softplus = -LRU_C * (jnp.maximum(nl, 0.0) + _log1p(jnp.exp(-jnp.abs(nl))))
    sub = lax.broadcasted_iota(jnp.int32, (ROW_ALIGN, D_MODEL), 0) & (SUBLANES - 1)
    carry = hcarry_ref[...]
    for r0 in range(0, tt, ROW_ALIGN):
        rows = slice(r0, r0 + ROW_ALIGN)
        log_a = neg_c_softplus * _sigmoid(pre_r[rows])
        a = jnp.exp(log_a)
        hs = jnp.sqrt(jnp.tanh(-log_a) * (a * a + 1.0)) * (_sigmoid(pre_i[rows]) * xc[rows])
        s = 1
        while s < SUBLANES:
            m = sub >= s
            hs = jnp.where(m, hs + a * pltpu.roll(hs, s, 0), hs)
            a = jnp.where(m, a * pltpu.roll(a, s, 0), a)
            s *= 2
        groups = []
        for g0 in range(0, ROW_ALIGN, SUBLANES):
            hg = hs[g0:g0 + SUBLANES] + a[g0:g0 + SUBLANES] * carry
            carry = hg[SUBLANES - 1:SUBLANES]
            groups.append(hg)
        y = yb[rows]
        gelu = 0.5 * y * (1.0 + jnp.tanh(SQRT_2_OVER_PI * (y + 0.044715 * (y * y * y))))
        out_ref[rows, :] = (jnp.concatenate(groups, axis=0) * gelu).astype(out_ref.dtype)
    hcarry_ref[...] = carry


def _inproj_kernel(tail_rows, x_ref, meta_ref, g_ref, wa_ref, wlru_ref, wgate_ref, mu_ref, cw_ref, cb_ref, lwa_ref,
                   ba_ref, lwx_ref, bx_ref, lam_ref, xs_ref, lru_ref, gates_ref, pcarry_ref, xtail_ref, hcarry_ref,
                   pa_ref):
    tt = x_ref.shape[1]

    @pl.when(pl.program_id(1) == 0)
    def _():
        pcarry_ref[...] = jnp.zeros_like(pcarry_ref)
        xtail_ref[...] = jnp.zeros_like(xtail_ref)
        hcarry_ref[...] = jnp.zeros_like(hcarry_ref)

    t = pl.program_id(1)
    blk = x_ref[0]
    first =jnp.concatenate([meta_ref[...], blk[:tt - N_META]], axis=0)
    if tail_rows < tt:
        last = jnp.concatenate([blk[tt - tail_rows:], jnp.zeros((tt - tail_rows, D_MODEL), F32)], axis=0)
    else:
        last = blk
    h = jnp.where(t == 0, first, jnp.where(t == pl.num_programs(1) - 1, last, blk))
    u = _rmsnorm(h, g_ref[...]).astype(BF16)

    pl_ = jnp.dot(u, wlru_ref[...], preferred_element_type=F32)
    xc = _causal_conv(pl_[:, 0:D_MODEL], cw_ref, cb_ref, xtail_ref)
    pa_ref[:, 0:D_MODEL] = jnp.dot(u, wa_ref[:, 0:D_MODEL], preferred_element_type=F32)
    xcb = xc.astype(BF16)
    pre_r = _block_diag_dot(xcb, lwa_ref) + ba_ref[...]
    pre_i = _block_diag_dot(xcb, lwx_ref) + bx_ref[...]
    pa_ref[:, D_MODEL:COLS_A] = jnp.dot(u, wa_ref[:, D_MODEL:COLS_A], preferred_element_type=F32)
    gates_ref[...] = jnp.dot(u, wgate_ref[...], preferred_element_type=F32)
    _rglru_recurrence(xc, pl_[:, D_MODEL:2 * D_MODEL], pre_r, pre_i, lam_ref, hcarry_ref, lru_ref)

    pa = pa_ref[...]
    row = lax.broadcasted_iota(jnp.int32, pa.shape, 0)
    prev = jnp.where(row == 0, pcarry_ref[...], pltpu.roll(pa, 1, 0))
    pcarry_ref[...] = pa[tt - 1:tt, :]
    xs_ref[...] = pa + (prev - pa) * mu_ref[...]


def _inproj(x, meta, t_pad, g, wa, wlru, wgate, mu, conv_w, conv_b, lwa, ba, lwx, bx, lam):
    batch, seq, _ = x.shape
    tt = TIME_TILE
    nt = t_pad // tt
    assert nt >= 2 and seq >= tt and seq % ROW_ALIGN == 0 and tt % ROW_ALIGN == 0 and N_META % ROW_ALIGN == 0
    tail_rows = N_META + seq - (nt - 1) * tt
    const = lambda b, t: (0, 0)
    rows = lambda b, t: (b * nt + t, 0)
    single = pl.Buffered(1)
    vec = pl.BlockSpec((1, D_MODEL), const)
    bd = pl.BlockSpec((D_MODEL // MXU_TILE, MXU_TILE, MXU_TILE), lambda b, t: (0, 0, 0))
    m = batch * t_pad
    return pl.pallas_call(
        functools.partial(_inproj_kernel, tail_rows),
        grid=(batch, nt),
        in_specs=[
            pl.BlockSpec((pl.Element(1), pl.Element(tt), pl.Element(D_MODEL)),
                         lambda b, t: (b, ROW_ALIGN * jnp.clip((tt // ROW_ALIGN) * t - N_META // ROW_ALIGN, 0,
                                                               (seq - tt) // ROW_ALIGN), 0)),
            pl.BlockSpec((N_META, D_MODEL), const),
            vec,
            pl.BlockSpec((D_MODEL, COLS_A), const, pipeline_mode=single),
            pl.BlockSpec((D_MODEL, 2 * D_MODEL), const, pipeline_mode=single),
            pl.BlockSpec((D_MODEL, 2 * D_MODEL), const, pipeline_mode=single),
            pl.BlockSpec((1, COLS_A), const),
            pl.BlockSpec((CONV_WIDTH, D_MODEL), const),
            vec, bd, vec, bd, vec, vec,
        ],
        out_specs=[
            pl.BlockSpec((tt, COLS_A), rows),
            pl.BlockSpec((tt, D_MODEL), rows),
            pl.BlockSpec((tt, 2 * D_MODEL), rows),
        ],
        out_shape=[jax.ShapeDtypeStruct((m, COLS_A), F32), jax.ShapeDtypeStruct((m, D_MODEL), BF16),
                   jax.ShapeDtypeStruct((m, 2 * D_MODEL), F32)],
        scratch_shapes=[
            pltpu.VMEM((1, COLS_A), F32),
            pltpu.VMEM((SUBLANES, D_MODEL), F32),
            pltpu.VMEM((1, D_MODEL), F32),
            pltpu.VMEM((tt, COLS_A), F32),
        ],
        compiler_params=pltpu.CompilerParams(dimension_semantics=("arbitrary", "arbitrary"),
                                             vmem_limit_bytes=VMEM_LIMIT),
        name="inproj_rglru",
    )(x, meta, g, wa, wlru, wgate, mu, conv_w, conv_b, lwa, ba, lwx, bx, lam)


def _wkv_terms(units):
    c = CHUNK
    f32dot = functools.partial(jnp.dot, preferred_element_type=F32)
    row = lax.broadcasted_iota(jnp.int32, (c, LANES), 0)
    lane = lax.broadcasted_iota(jnp.int32, (c, LANES), 1)
    low = lane < HEAD
    col = lane & (HEAD - 1)
    strict = col < row
    incl = col <= row
    eye_hi = jnp.where(lane - HEAD == row, 1.0, 0.0).astype(BF16)
    zeros = jnp.zeros((c, LANES), BF16)
    at, at_hi, bt, kt, rt, rt_hi, v, bh_hi, kh_hi = (list(x) for x in zip(*units))

    gq = [_dot_nt(jnp.concatenate([a, r], axis=0), jnp.concatenate([b, k], axis=0))
          for a, r, b, k in zip(at, rt, bt, kt)]
    g = [jnp.where(strict, x[:c], 0.0).astype(BF16) for x in gq]
    q = [jnp.where(incl, x[c:], 0.0).astype(BF16) for x in gq]
    x0 = [jnp.where(low, f32dot(y, jnp.concatenate([zeros, vv], axis=0)).astype(BF16), a)
          for y, vv, a in zip(g, v, at_hi)]
    wb = [jnp.where(low, y, eye_hi) for y in g]
    n_fac = int(math.ceil(math.log2(c)))
    for i in range(n_fac):
        pw = [f32dot(x[:, :HEAD], x).astype(BF16) for x in wb]
        if i + 1 < n_fac:
            wb = [jnp.where(low, y, x + y) for x, y in zip(wb, pw)]
        else:
            wb = [x + y for x, y in zip(wb, pw)]
    tu = [f32dot(x, jnp.concatenate([zeros, y], axis=0)).astype(BF16) for x, y in zip(wb, x0)]
    rhs2 = [jnp.concatenate([x, vv], axis=0) for x, vv in zip(tu, v)]
    ry = [f32dot(x, y) for x, y in zip(q, rhs2)]
    rp = [jnp.where(low, zeros, r + y.astype(BF16)) for r, y in zip(rt_hi, ry)]
    mc = [_dot_tn(y, jnp.concatenate([b, k], axis=0)) for y, b, k in zip(rhs2, bh_hi, kh_hi)]
    return rp, ry, mc


def _rwkv_kernel(xs_ref, w0_ref, wdu_ref, a0_ref, wau_ref, wgu_ref, kk_ref, ka_ref, rk_ref,
                 lnw_ref, lnb_ref, ones_ref, out_ref,
                 state_ref, at_s, ath_s, bt_s, kt_s, rt_s, rth_s, v_s, bh_s, kh_s, wl_s, y_s, yfull_ref):
    tt = xs_ref.shape[0]
    nc = tt // CHUNK

    @pl.when(pl.program_id(1) == 0)
    def _():
        state_ref[...] = jnp.zeros_like(state_ref)

    r = xs_ref[:, 0:D_MODEL]
    k = xs_ref[:, D_MODEL:2 * D_MODEL]
    v = xs_ref[:, 2 * D_MODEL:3 * D_MODEL]
    o = 3 * D_MODEL
    wd = xs_ref[:, o:o + DECAY_LORA]
    ad = xs_ref[:, o + DECAY_LORA:o + DECAY_LORA + AAA_LORA]
    gd = xs_ref[:, o + DECAY_LORA + AAA_LORA:COLS_A]

    lw = -EXP_M05 * _sigmoid(w0_ref[...] + _dot(jnp.tanh(wd), wdu_ref[...]))
    lr = _sigmoid(a0_ref[...] + _dot(ad, wau_ref[...]))
    g = _dot(_sigmoid(gd), wgu_ref[...])

    ones_bd = ones_ref[...]

    def head_sum(x):
        xb = x.astype(BF16)
        return jnp.concatenate(
            [jnp.dot(xb[:, j:j + MXU_TILE], ones_bd, preferred_element_type=F32)
             for j in range(0, D_MODEL, MXU_TILE)], axis=1)

    kk = k * kk_ref[...]
    kk = kk * lax.rsqrt(jnp.maximum(head_sum(kk * kk), 1e-24))
    kmod = k * (1.0 + (lr - 1.0) * ka_ref[...])
    bvec = kk * lr

    ri = lax.broadcasted_iota(jnp.int32, (tt, tt), 0)
    ci = lax.broadcasted_iota(jnp.int32, (tt, tt), 1)
    tri = jnp.where(((ri // CHUNK) == (ci // CHUNK)) & (ci <= ri), 1.0, 0.0).astype(BF16)
    cw = _dot_split(tri, lw)
    tot = jnp.concatenate([jnp.broadcast_to(cw[c0 + CHUNK - 1:c0 + CHUNK, :], (CHUNK, D_MODEL))
                           for c0 in range(0, tt, CHUNK)], axis=0)

    w_inc = jnp.exp(cw)
    w_inv = jnp.exp(-cw)
    w_prev = jnp.exp(cw - lw)
    w_end = jnp.exp(tot - cw)
    w_tot = jnp.exp(tot)

    at_f = -kk * w_prev
    bt_f = bvec * w_inv
    kt_f = kmod * w_inv
    rt_f = r * w_inc
    bh_f = bvec * w_end
    kh_f = kmod * w_end
    lane = lax.broadcasted_iota(jnp.int32, (tt, LANES), 1)
    low = lane < HEAD
    zero = jnp.zeros((tt, LANES), BF16)

    def pair_views(x_f, j, dtype=BF16):
        blk = x_f[:, j * LANES:(j + 1) * LANES].astype(dtype)
        return blk, pltpu.roll(blk, HEAD, 1)

    def lo_masked(blk, swapped):
        return jnp.where(low, blk, zero), jnp.where(low, swapped, zero)

    def hi_masked(blk, swapped):
        return jnp.where(low, zero, swapped), jnp.where(low, zero, blk)

    for j in range(N_HEADS // 2):
        at_b, at_sw = pair_views(at_f, j)
        rt_b, rt_sw = pair_views(rt_f, j)
        wl_b, wl_sw = pair_views(w_tot, j, F32)
        stores = ((at_s, (at_b, at_sw)), (ath_s, (at_sw, at_b)),
                  (bt_s, lo_masked(*pair_views(bt_f, j))), (kt_s, lo_masked(*pair_views(kt_f, j))),
                  (rt_s, (rt_b, rt_sw)), (rth_s, (rt_sw, rt_b)),
                  (v_s, lo_masked(*pair_views(v, j))),
                  (bh_s, hi_masked(*pair_views(bh_f, j))), (kh_s, hi_masked(*pair_views(kh_f, j))),
                  (wl_s, (wl_sw, wl_b)))
        for ref, (even, odd) in stores:
            ref[2 * j] = even
            ref[2 * j + 1] = odd

    for h0 in range(0, N_HEADS, HEADS_PER_GROUP):
        heads = range(h0, h0 + HEADS_PER_GROUP)
        units = []
        for h in heads:
            for c in range(nc):
                rows = pl.ds(c * CHUNK, CHUNK)
                units.append(tuple(ref[h, rows, :]
                                   for ref in (at_s, ath_s, bt_s, kt_s, rt_s, rth_s, v_s, bh_s, kh_s)))
        rp, ry, mc = _wkv_terms(units)
        s = [state_ref[h] for h in heads]
        for c in range(nc):
            rows = pl.ds(c * CHUNK, CHUNK)
            sb = [x.astype(BF16) for x in s]
            for j, h in enumerate(heads):
                u = j * nc + c
                y_s[h, rows, :] = _dot_nt(rp[u], sb[j]) + ry[u][:, :HEAD]
            s = [s[j] * wl_s[h, pl.ds(c * CHUNK, 1), :]
                 + jnp.dot(sb[j], mc[j * nc + c].astype(BF16), preferred_element_type=F32)
                 + mc[j * nc + c][:CHUNK] for j, h in enumerate(heads)]
        for j, h in enumerate(heads):
            state_ref[h] = s[j]

    for h in range(N_HEADS):
        yfull_ref[:, h * HEAD:(h + 1) * HEAD] = y_s[h]
    y = yfull_ref[...]

    inv_n = 1.0 / HEAD
    mean = head_sum(y) * inv_n
    yc = y - mean
    var = head_sum(yc * yc) * inv_n
    yn = yc * lax.rsqrt(var + LN_X_EPS) * lnw_ref[...] + lnb_ref[...]
    bonus = head_sum(r * kmod * rk_ref[...]) * v
    out_ref[...] = ((yn + bonus) * g).astype(out_ref.dtype)


def _rwkv(xs, batch, t_pad, w0, wdu, a0, wau, wgu, k_k, k_a, r_k, ln_w, ln_b, ones_bd):
    tt = TIME_TILE
    nt = t_pad // tt
    const = lambda b, t: (0, 0)
    vec = pl.BlockSpec((1, D_MODEL), const)
    head_bf = pltpu.VMEM((N_HEADS, tt, LANES), BF16)
    return pl.pallas_call(
        _rwkv_kernel,
        grid=(batch, nt),
        in_specs=[
            pl.BlockSpec((tt, COLS_A), lambda b, t: (b * nt + t, 0)),
            vec,
            pl.BlockSpec((DECAY_LORA, D_MODEL), const),
            vec,
            pl.BlockSpec((AAA_LORA, D_MODEL), const),
            pl.BlockSpec((GATE_LORA, D_MODEL), const),
            vec, vec, vec, vec, vec,
            pl.BlockSpec((MXU_TILE, MXU_TILE), const),
        ],
        out_specs=pl.BlockSpec((tt, D_MODEL), lambda b, t: (b * nt + t, 0)),
        out_shape=jax.ShapeDtypeStruct((batch * t_pad, D_MODEL), BF16),
        scratch_shapes=[
            pltpu.VMEM((N_HEADS, HEAD, LANES), F32),
            head_bf, head_bf, head_bf, head_bf, head_bf, head_bf, head_bf, head_bf, head_bf,
            pltpu.VMEM((N_HEADS, tt, LANES), F32),
            pltpu.VMEM((N_HEADS, tt, HEAD), F32),
            pltpu.VMEM((tt, D_MODEL), F32),
        ],
        compiler_params=pltpu.CompilerParams(dimension_semantics=("arbitrary", "arbitrary"),
                                             vmem_limit_bytes=VMEM_LIMIT),
        name="rwkv7_mix",
    )(xs, w0, wdu, a0, wau, wgu, k_k, k_a, r_k, ln_w, ln_b, ones_bd)


def _out_kernel(x_ref, yg_ref, lru_ref, gates_ref, wpa_ref, wpb_ref, wout_ref, gffn_ref, wup_ref, wdown_ref,
                gfin_ref, out_ref):
    ya = jnp.dot(yg_ref[0], wpa_ref[...], preferred_element_type=F32)
    yr = jnp.dot(lru_ref[0], wpb_ref[...], preferred_element_type=F32)
    ga = _sigmoid(gates_ref[0, :, 0:D_MODEL])
    gb = _sigmoid(gates_ref[0, :, D_MODEL:2 * D_MODEL])
    h1 = x_ref[...] + _dot(ga * ya + gb * yr, wout_ref[...])
    u = _rmsnorm(h1, gffn_ref[...]).astype(BF16)
    h2 = h1
    for c0 in range(0, wup_ref.shape[1], D_MODEL):
        z = jnp.dot(u, wup_ref[:, c0:c0 + D_MODEL], preferred_element_type=F32)
        z = jnp.square(jnp.maximum(z, 0.0))
        h2 = h2 + _dot(z, wdown_ref[c0:c0 + D_MODEL, :])
    out_ref[...] = _rmsnorm(h2, gfin_ref[...])


def _out(x, yg, lru, gates, wpa, wpb, wout, gffn, wup, wdown, gfin):
    batch, seq, _ = x.shape
    d_ff = wup.shape[1]
    tile = math.gcd(seq, OUT_TILE)
    const = lambda b, j: (0, 0)
    rows = lambda b, j: (b, j, 0)
    shifted = lambda b, j: (b, ROW_ALIGN * ((tile // ROW_ALIGN) * j + N_META // ROW_ALIGN), 0)
    single = pl.Buffered(1)
    vec = pl.BlockSpec((1, D_MODEL), const)
    mat = pl.BlockSpec((D_MODEL, D_MODEL), const, pipeline_mode=single)
    return pl.pallas_call(
        _out_kernel,
        grid=(batch, seq // tile),
        in_specs=[
            pl.BlockSpec((None, tile, D_MODEL), rows),
            pl.BlockSpec((pl.Element(1), pl.Element(tile), pl.Element(D_MODEL)), shifted),
            pl.BlockSpec((pl.Element(1), pl.Element(tile), pl.Element(D_MODEL)), shifted),
            pl.BlockSpec((pl.Element(1), pl.Element(tile), pl.Element(2 * D_MODEL)), shifted),
            mat, mat, mat, vec,
            pl.BlockSpec((D_MODEL, d_ff), const, pipeline_mode=single),
            pl.BlockSpec((d_ff, D_MODEL), const, pipeline_mode=single),
            vec,
        ],
        out_specs=pl.BlockSpec((None, tile, D_MODEL), rows),
        out_shape=jax.ShapeDtypeStruct((batch, seq, D_MODEL), F32),
        compiler_params=pltpu.CompilerParams(dimension_semantics=("arbitrary", "arbitrary"),
                                             vmem_limit_bytes=VMEM_LIMIT),
        name="outproj_mlp",
    )(x, yg, lru, gates, wpa, wpb, wout, gffn, wup, wdown, gfin)


def _block_diag_tiles(w):
    n, b, _ = w.shape
    per = MXU_TILE // b
    w = w.reshape(n // per, per, b, b)
    eye = jnp.eye(per, dtype=w.dtype)
    return (eye[None, :, None, :, None] * w[:, :, :, None, :]).reshape(n // per, MXU_TILE, MXU_TILE)


def kernel(x, meta_tokens, norm_mix_g, w_in, mu_shift, w0, w_decay_up, a0, w_aaa_up, w_gate_up, k_k, k_a, r_k,
           ln_x_w, ln_x_b, w_proj_a, conv_w, conv_b, lru_wa, lru_ba, lru_wx, lru_bx, lru_lambda, w_proj_b, w_out,
           norm_ffn_g, w_ff_up, w_ff_down, norm_final_g):
    batch, seq, d = x.shape
    assert d == D_MODEL and norm_mix_g.shape[0] == 1
    t_real = N_META + seq
    t_pad = -(-t_real // TIME_TILE) * TIME_TILE

    row = lambda p: p.reshape(1, -1).astype(F32)
    w_in_bf = w_in[0].astype(BF16)
    c_lru, c_gate = COLS_A, COLS_A + 2 * D_MODEL
    xs, lru, gates = _inproj(
        x, meta_tokens.astype(x.dtype), t_pad, row(norm_mix_g[0]), w_in_bf[:, :c_lru], w_in_bf[:, c_lru:c_gate], w_in_bf[:, c_gate:],
        row(mu_shift[0]), conv_w[0].astype(F32), row(conv_b[0]), _block_diag_tiles(lru_wa[0]).astype(BF16),
        row(lru_ba[0]), _block_diag_tiles(lru_wx[0]).astype(BF16), row(lru_bx[0]), row(lru_lambda[0]))

    ones_bd = _block_diag_tiles(jnp.ones((MXU_TILE // HEAD, HEAD, HEAD), BF16))[0]
    yg = _rwkv(xs, batch, t_pad, row(w0[0]), w_decay_up[0].astype(BF16), row(a0[0]),
               w_aaa_up[0].astype(BF16), w_gate_up[0].astype(BF16), row(k_k[0]), row(k_a[0]), row(r_k[0]),
               row(ln_x_w[0]), row(ln_x_b[0]), ones_bd)

    seq3 = lambda a: a.reshape(batch, t_pad, a.shape[-1])
    return _out(x, seq3(yg), seq3(lru), seq3(gates), w_proj_a[0].astype(BF16), w_proj_b[0].astype(BF16),
                w_out[0].astype(BF16), row(norm_ffn_g[0]), w_ff_up[0].astype(BF16), w_ff_down[0].astype(BF16),
                row(norm_final_g))
```

```python
import functools
import math

import jax
import jax.numpy as jnp
from jax import lax
from jax.experimental import pallas as pl
from jax.experimental.pallas import tpu as pltpu

F32 = jnp.float32
BF16 = jnp.bfloat16

D_MODEL = 1024
N_META = 16
HEAD = 64
N_HEADS = D_MODEL // HEAD
DECAY_LORA = 64
AAA_LORA = 64
GATE_LORA = 128
COLS_A = 3 * D_MODEL + DECAY_LORA + AAA_LORA + GATE_LORA
LN_X_EPS = 64e-5
RMS_EPS = 1e-6
LRU_C = 8.0
CONV_WIDTH = 4
EXP_M05 = math.exp(-0.5)
SQRT_2_OVER_PI = math.sqrt(2.0 / math.pi)

MXU_TILE = 256
SUBLANES = 8
CHUNK = 64
TIME_TILE = 192
LANES = 128
HEADS_PER_GROUP = 16
OUT_TILE = 512
ROW_ALIGN = 16
VMEM_LIMIT = 60 * 1024 * 1024

NT_DIMS = (((1,), (1,)), ((), ()))
TN_DIMS = (((0,), (0,)), ((), ()))


def _sigmoid(x):
    return 0.5 * jnp.tanh(0.5 * x) + 0.5


def _log1p(z):
    u = 1.0 + z
    return jnp.where(u == 1.0, z, jnp.log(u) * z / jnp.where(u == 1.0, 1.0, u - 1.0))


def _dot(a, b):
    return jnp.dot(a.astype(BF16), b.astype(BF16), preferred_element_type=F32)


def _dot_nt(a, b):
    return lax.dot_general(a.astype(BF16), b.astype(BF16), NT_DIMS, preferred_element_type=F32)


def _dot_tn(a, b):
    return lax.dot_general(a.astype(BF16), b.astype(BF16), TN_DIMS, preferred_element_type=F32)


def _dot_split(m, x):
    hi = x.astype(BF16)
    lo = (x - hi.astype(F32)).astype(BF16)
    return (jnp.dot(m, hi, preferred_element_type=F32) + jnp.dot(m, lo, preferred_element_type=F32))


def _rmsnorm(x, g):
    return x * lax.rsqrt(jnp.mean(x * x, axis=-1, keepdims=True) + RMS_EPS) * g


def _causal_conv(xb, cw_ref, cb_ref, xtail_ref):
    tt = xb.shape[0]
    tail = xtail_ref[...]
    row8 = lax.broadcasted_iota(jnp.int32, (SUBLANES, D_MODEL), 0)
    xc = cb_ref[...] + xb * cw_ref[CONV_WIDTH - 1:CONV_WIDTH, :]
    for s in range(1, CONV_WIDTH):
        sh = pltpu.roll(xb, s, 0)
        head = jnp.where(row8 < s, pltpu.roll(tail, s, 0), sh[0:SUBLANES])
        sh = jnp.concatenate([head, sh[SUBLANES:]], axis=0)
        xc = xc + sh * cw_ref[CONV_WIDTH - 1 - s:CONV_WIDTH - s, :]
    xtail_ref[...] = xb[tt - SUBLANES:tt]
    return xc


def _block_diag_dot(xb16, w_ref):
    return jnp.concatenate(
        [jnp.dot(xb16[:, j * MXU_TILE:(j + 1) * MXU_TILE], w_ref[j], preferred_element_type=F32)
         for j in range(D_MODEL // MXU_TILE)], axis=1)


def _rglru_recurrence(xc, yb, pre_r, pre_i, lam_ref, hcarry_ref, out_ref):
    tt = xc.shape[0]
    nl = -lam_ref[...]
    neg_c_softplus = -LRU_C * (jnp.maximum(nl, 0.0) + _log1p(jnp.exp(-jnp.abs(nl))))
    sub = lax.broadcasted_iota(jnp.int32, (ROW_ALIGN, D_MODEL), 0) & (SUBLANES - 1)
    carry = hcarry_ref[...]
    for r0 in range(0, tt, ROW_ALIGN):
        rows = slice(r0, r0 + ROW_ALIGN)
        log_a = neg_c_softplus * _sigmoid(pre_r[rows])
        a = jnp.exp(log_a)
        hs = jnp.sqrt(jnp.tanh(-log_a) * (a * a + 1.0)) * (_sigmoid(pre_i[rows]) * xc[rows])
        s = 1
        while s < SUBLANES:
            m = sub >= s
            hs = jnp.where(m, hs + a * pltpu.roll(hs, s, 0), hs)
            a = jnp.where(m, a * pltpu.roll(a, s, 0), a)
            s *= 2
        groups = []
        for g0 in range(0, ROW_ALIGN, SUBLANES):
            hg = hs[g0:g0 + SUBLANES] + a[g0:g0 + SUBLANES] * carry
            carry = hg[SUBLANES - 1:SUBLANES]
            groups.append(hg)
        y = yb[rows]
        gelu = 0.5 * y * (1.0 + jnp.tanh(SQRT_2_OVER_PI * (y + 0.044715 * (y * y * y))))
        out_ref[rows, :] = (jnp.concatenate(groups, axis=0) * gelu).astype(out_ref.dtype)
    hcarry_ref[...] = carry


def _inproj_kernel(tail_rows, x_ref, meta_ref, g_ref, wa_ref, wlru_ref, wgate_ref, mu_ref, cw_ref, cb_ref, lwa_ref,
                   ba_ref, lwx_ref, bx_ref, lam_ref, xs_ref, lru_ref, gates_ref, pcarry_ref, xtail_ref, hcarry_ref,
                   pa_ref):
    tt = x_ref.shape[1]

    @pl.when(pl.program_id(1) == 0)
    def _():
        pcarry_ref[...] = jnp.zeros_like(pcarry_ref)
        xtail_ref[...] = jnp.zeros_like(xtail_ref)
        hcarry_ref[...] = jnp.zeros_like(hcarry_ref)

    t = pl.program_id(1)
    blk = x_ref[0]
    first = jnp.concatenate([meta_ref[...], blk[:tt - N_META]], axis=0)
    if tail_rows < tt:
        last = jnp.concatenate([blk[tt - tail_rows:], jnp.zeros((tt - tail_rows, D_MODEL), F32)], axis=0)
    else:
        last = blk
    h = jnp.where(t == 0, first, jnp.where(t == pl.num_programs(1) - 1, last, blk))
    u = _rmsnorm(h, g_ref[...]).astype(BF16)

    pl_ = jnp.dot(u, wlru_ref[...], preferred_element_type=F32)
    xc = _causal_conv(pl_[:, 0:D_MODEL], cw_ref, cb_ref, xtail_ref)
    pa_ref[:, 0:D_MODEL] = jnp.dot(u, wa_ref[:, 0:D_MODEL], preferred_element_type=F32)
    xcb = xc.astype(BF16)
    pre_r = _block_diag_dot(xcb, lwa_ref) + ba_ref[...]
    pre_i = _block_diag_dot(xcb, lwx_ref) + bx_ref[...]
    pa_ref[:, D_MODEL:COLS_A] = jnp.dot(u, wa_ref[:, D_MODEL:COLS_A], preferred_element_type=F32)
    gates_ref[...] = jnp.dot(u, wgate_ref[...], preferred_element_type=F32)
    _rglru_recurrence(xc, pl_[:, D_MODEL:2 * D_MODEL], pre_r, pre_i, lam_ref, hcarry_ref, lru_ref)

    pa = pa_ref[...]
    row = lax.broadcasted_iota(jnp.int32, pa.shape, 0)
    prev = jnp.where(row == 0, pcarry_ref[...], pltpu.roll(pa, 1, 0))
    pcarry_ref[...] = pa[tt - 1:tt, :]
    xs_ref[...] = pa + (prev - pa) * mu_ref[...]


def _inproj(x, meta, t_pad, g, wa, wlru, wgate, mu, conv_w, conv_b, lwa, ba, lwx, bx, lam):
    batch, seq, _ = x.shape
    tt = TIME_TILE
    nt = t_pad // tt
    assert nt >= 2 and seq >= tt and seq % ROW_ALIGN == 0 and tt % ROW_ALIGN == 0 and N_META % ROW_ALIGN == 0
    tail_rows = N_META + seq - (nt - 1) * tt
    const = lambda b, t: (0, 0)
    rows = lambda b, t: (b * nt + t, 0)
    single = pl.Buffered(1)
    vec = pl.BlockSpec((1, D_MODEL), const)
    bd = pl.BlockSpec((D_MODEL // MXU_TILE, MXU_TILE, MXU_TILE), lambda b, t: (0, 0, 0))
    m = batch * t_pad
    return pl.pallas_call(
        functools.partial(_inproj_kernel, tail_rows),
        grid=(batch, nt),
        in_specs=[
            pl.BlockSpec((pl.Element(1), pl.Element(tt), pl.Element(D_MODEL)),
                         lambda b, t: (b, ROW_ALIGN * jnp.clip((tt // ROW_ALIGN) * t - N_META // ROW_ALIGN, 0,
                                                               (seq - tt) // ROW_ALIGN), 0)),
            pl.BlockSpec((N_META, D_MODEL), const),
            vec,
            pl.BlockSpec((D_MODEL, COLS_A), const, pipeline_mode=single),
            pl.BlockSpec((D_MODEL, 2 * D_MODEL), const, pipeline_mode=single),
            pl.BlockSpec((D_MODEL, 2 * D_MODEL), const, pipeline_mode=single),
            pl.BlockSpec((1, COLS_A), const),
            pl.BlockSpec((CONV_WIDTH, D_MODEL), const),
            vec, bd, vec, bd, vec, vec,
        ],
        out_specs=[
            pl.BlockSpec((tt, COLS_A), rows),
            pl.BlockSpec((tt, D_MODEL), rows),
            pl.BlockSpec((tt, 2 * D_MODEL), rows),
        ],
        out_shape=[jax.ShapeDtypeStruct((m, COLS_A), F32), jax.ShapeDtypeStruct((m, D_MODEL), BF16),
                   jax.ShapeDtypeStruct((m, 2 * D_MODEL), F32)],
        scratch_shapes=[
            pltpu.VMEM((1, COLS_A), F32),
            pltpu.VMEM((SUBLANES, D_MODEL), F32),
            pltpu.VMEM((1, D_MODEL), F32),
            pltpu.VMEM((tt, COLS_A), F32),
        ],
        compiler_params=pltpu.CompilerParams(dimension_semantics=("arbitrary", "arbitrary"),
                                             vmem_limit_bytes=VMEM_LIMIT),
        name="inproj_rglru",
    )(x, meta, g, wa, wlru, wgate, mu, conv_w, conv_b, lwa, ba, lwx, bx, lam)


def _wkv_terms(units):
    c = CHUNK
    f32dot = functools.partial(jnp.dot, preferred_element_type=F32)
    row = lax.broadcasted_iota(jnp.int32, (c, LANES), 0)
    lane = lax.broadcasted_iota(jnp.int32, (c, LANES), 1)
    low = lane < HEAD
    col = lane & (HEAD - 1)
    strict = col < row
    incl = col <= row
    eye_hi = jnp.where(lane - HEAD == row, 1.0, 0.0).astype(BF16)
    zeros = jnp.zeros((c, LANES), BF16)
    at, at_hi, bt, kt, rt, rt_hi, v, bh_hi, kh_hi = (list(x) for x in zip(*units))

    gq = [_dot_nt(jnp.concatenate([a, r], axis=0), jnp.concatenate([b, k], axis=0))
          for a, r, b, k in zip(at, rt, bt, kt)]
    g = [jnp.where(strict, x[:c], 0.0).astype(BF16) for x in gq]
    q = [jnp.where(incl, x[c:], 0.0).astype(BF16) for x in gq]
    x0 = [jnp.where(low, f32dot(y, jnp.concatenate([zeros, vv], axis=0)).astype(BF16), a)
          for y, vv, a in zip(g, v, at_hi)]
    wb = [jnp.where(low, y, eye_hi) for y in g]
    n_fac = int(math.ceil(math.log2(c)))
    for i in range(n_fac):
        pw = [f32dot(x[:, :HEAD], x).astype(BF16) for x in wb]
        if i + 1 < n_fac:
            wb = [jnp.where(low, y, x + y) for x, y in zip(wb, pw)]
        else:
            wb = [x + y for x, y in zip(wb, pw)]
    tu = [f32dot(x, jnp.concatenate([zeros, y], axis=0)).astype(BF16) for x, y in zip(wb, x0)]
    rhs2 = [jnp.concatenate([x, vv], axis=0) for x, vv in zip(tu, v)]
    ry = [f32dot(x, y) for x, y in zip(q, rhs2)]
    rp = [jnp.where(low, zeros, r + y.astype(BF16)) for r, y in zip(rt_hi, ry)]
    mc = [_dot_tn(y, jnp.concatenate([b, k], axis=0)) for y, b, k in zip(rhs2, bh_hi, kh_hi)]
    return rp, ry, mc


def _rwkv_kernel(xs_ref, w0_ref, wdu_ref, a0_ref, wau_ref, wgu_ref, kk_ref, ka_ref, rk_ref,
                 lnw_ref, lnb_ref, ones_ref, out_ref,
                 state_ref, at_s, ath_s, bt_s, kt_s, rt_s, rth_s, v_s, bh_s, kh_s, wl_s, y_s, yfull_ref):
    tt = xs_ref.shape[0]
    nc = tt // CHUNK

    @pl.when(pl.program_id(1) == 0)
    def _():
        state_ref[...] = jnp.zeros_like(state_ref)

    r = xs_ref[:, 0:D_MODEL]
    k = xs_ref[:, D_MODEL:2 * D_MODEL]
    v = xs_ref[:, 2 * D_MODEL:3 * D_MODEL]
    o = 3 * D_MODEL
    wd = xs_ref[:, o:o + DECAY_LORA]
    ad = xs_ref[:, o + DECAY_LORA:o + DECAY_LORA + AAA_LORA]
    gd = xs_ref[:, o + DECAY_LORA + AAA_LORA:COLS_A]

    lw = -EXP_M05 * _sigmoid(w0_ref[...] + _dot(jnp.tanh(wd), wdu_ref[...]))
    lr = _sigmoid(a0_ref[...] + _dot(ad, wau_ref[...]))
    g = _dot(_sigmoid(gd), wgu_ref[...])

    ones_bd = ones_ref[...]

    def head_sum(x):
        xb = x.astype(BF16)
        return jnp.concatenate(
            [jnp.dot(xb[:, j:j + MXU_TILE], ones_bd, preferred_element_type=F32)
             for j in range(0, D_MODEL, MXU_TILE)], axis=1)

    kk = k * kk_ref[...]
    kk = kk * lax.rsqrt(jnp.maximum(head_sum(kk * kk), 1e-24))
    kmod = k * (1.0 + (lr - 1.0) * ka_ref[...])
    bvec = kk * lr

    ri = lax.broadcasted_iota(jnp.int32, (tt, tt), 0)
    ci = lax.broadcasted_iota(jnp.int32, (tt, tt), 1)
    tri = jnp.where(((ri // CHUNK) == (ci // CHUNK)) & (ci <= ri), 1.0, 0.0).astype(BF16)
    cw = _dot_split(tri, lw)
    tot = jnp.concatenate([jnp.broadcast_to(cw[c0 + CHUNK - 1:c0 + CHUNK, :], (CHUNK, D_MODEL))
                           for c0 in range(0, tt, CHUNK)], axis=0)

    w_inc = jnp.exp(cw)
    w_inv = jnp.exp(-cw)
    w_prev = jnp.exp(cw - lw)
    w_end = jnp.exp(tot - cw)
    w_tot = jnp.exp(tot)

    at_f = -kk * w_prev
    bt_f = bvec * w_inv
    kt_f = kmod * w_inv
    rt_f = r * w_inc
    bh_f = bvec * w_end
    kh_f = kmod * w_end
    lane = lax.broadcasted_iota(jnp.int32, (tt, LANES), 1)
    low = lane < HEAD
    zero = jnp.zeros((tt, LANES), BF16)

    def pair_views(x_f, j, dtype=BF16):
        blk = x_f[:, j * LANES:(j + 1) * LANES].astype(dtype)
        return blk, pltpu.roll(blk, HEAD, 1)

    def lo_masked(blk, swapped):
        return jnp.where(low, blk, zero), jnp.where(low, swapped, zero)

    def hi_masked(blk, swapped):
        return jnp.where(low, zero, swapped), jnp.where(low, zero, blk)

    for j in range(N_HEADS // 2):
        at_b, at_sw = pair_views(at_f, j)
        rt_b, rt_sw = pair_views(rt_f, j)
        wl_b, wl_sw = pair_views(w_tot, j, F32)
        stores = ((at_s, (at_b, at_sw)), (ath_s, (at_sw, at_b)),
                  (bt_s, lo_masked(*pair_views(bt_f, j))), (kt_s, lo_masked(*pair_views(kt_f, j))),
                  (rt_s, (rt_b, rt_sw)), (rth_s, (rt_sw, rt_b)),
                  (v_s, lo_masked(*pair_views(v, j))),
                  (bh_s, hi_masked(*pair_views(bh_f, j))), (kh_s, hi_masked(*pair_views(kh_f, j))),
                  (wl_s, (wl_sw, wl_b)))
        for ref, (even, odd) in stores:
            ref[2 * j] = even
            ref[2 * j + 1] = odd

    for h0 in range(0, N_HEADS, HEADS_PER_GROUP):
        heads = range(h0, h0 + HEADS_PER_GROUP)
        units = []
        for h in heads:
            for c in range(nc):
                rows = pl.ds(c * CHUNK, CHUNK)
                units.append(tuple(ref[h, rows, :]
                                   for ref in (at_s, ath_s, bt_s, kt_s, rt_s, rth_s, v_s, bh_s, kh_s)))
        rp, ry, mc = _wkv_terms(units)
        s = [state_ref[h] for h in heads]
        for c in range(nc):
            rows = pl.ds(c * CHUNK, CHUNK)
            sb = [x.astype(BF16) for x in s]
            for j, h in enumerate(heads):
                u = j * nc + c
                y_s[h, rows, :] = _dot_nt(rp[u], sb[j]) + ry[u][:, :HEAD]
            s = [s[j] * wl_s[h, pl.ds(c * CHUNK, 1), :]
                 + jnp.dot(sb[j], mc[j * nc + c].astype(BF16), preferred_element_type=F32)
                 + mc[j * nc + c][:CHUNK] for j, h in enumerate(heads)]
        for j, h in enumerate(heads):
            state_ref[h] = s[j]

    for h in range(N_HEADS):
        yfull_ref[:, h * HEAD:(h + 1) * HEAD] = y_s[h]
    y = yfull_ref[...]

    inv_n = 1.0 / HEAD
    mean = head_sum(y) * inv_n
    yc = y - mean
    var = head_sum(yc * yc) * inv_n
    yn = yc * lax.rsqrt(var + LN_X_EPS) * lnw_ref[...] + lnb_ref[...]
    bonus = head_sum(r * kmod * rk_ref[...]) * v
    out_ref[...] = ((yn + bonus) * g).astype(out_ref.dtype)


def _rwkv(xs, batch, t_pad, w0, wdu, a0, wau, wgu, k_k, k_a, r_k, ln_w, ln_b, ones_bd):
    tt = TIME_TILE
    nt = t_pad // tt
    const = lambda b, t: (0, 0)
    vec = pl.BlockSpec((1, D_MODEL), const)
    head_bf = pltpu.VMEM((N_HEADS, tt, LANES), BF16)
    return pl.pallas_call(
        _rwkv_kernel,
        grid=(batch, nt),
        in_specs=[
            pl.BlockSpec((tt, COLS_A), lambda b, t: (b * nt + t, 0)),
            vec,
            pl.BlockSpec((DECAY_LORA, D_MODEL), const),
            vec,
            pl.BlockSpec((AAA_LORA, D_MODEL), const),
            pl.BlockSpec((GATE_LORA, D_MODEL), const),
            vec, vec, vec, vec, vec,
            pl.BlockSpec((MXU_TILE, MXU_TILE), const),
        ],
        out_specs=pl.BlockSpec((tt, D_MODEL), lambda b, t: (b * nt + t, 0)),
        out_shape=jax.ShapeDtypeStruct((batch * t_pad, D_MODEL), BF16),
        scratch_shapes=[
            pltpu.VMEM((N_HEADS, HEAD, LANES), F32),
            head_bf, head_bf, head_bf, head_bf, head_bf, head_bf, head_bf, head_bf, head_bf,
            pltpu.VMEM((N_HEADS, tt, LANES), F32),
            pltpu.VMEM((N_HEADS, tt, HEAD), F32),
            pltpu.VMEM((tt, D_MODEL), F32),
        ],
        compiler_params=pltpu.CompilerParams(dimension_semantics=("arbitrary", "arbitrary"),
                                             vmem_limit_bytes=VMEM_LIMIT),
        name="rwkv7_mix",
    )(xs, w0, wdu, a0, wau, wgu, k_k, k_a, r_k, ln_w, ln_b, ones_bd)


def _out_kernel(x_ref, yg_ref, lru_ref, gates_ref, wpa_ref, wpb_ref, wout_ref, gffn_ref, wup_ref, wdown_ref,
                gfin_ref, out_ref):
    ya = jnp.dot(yg_ref[0], wpa_ref[...], preferred_element_type=F32)
    yr = jnp.dot(lru_ref[0], wpb_ref[...], preferred_element_type=F32)
    ga = _sigmoid(gates_ref[0, :, 0:D_MODEL])
    gb = _sigmoid(gates_ref[0, :, D_MODEL:2 * D_MODEL])
    h1 = x_ref[...] + _dot(ga * ya + gb * yr, wout_ref[...])
    u = _rmsnorm(h1, gffn_ref[...]).astype(BF16)
    h2 = h1
    for c0 in range(0, wup_ref.shape[1], D_MODEL):
        z = jnp.dot(u, wup_ref[:, c0:c0 + D_MODEL], preferred_element_type=F32)
        z = jnp.square(jnp.maximum(z, 0.0))
        h2 = h2 + _dot(z, wdown_ref[c0:c0 + D_MODEL, :])
    out_ref[...] = _rmsnorm(h2, gfin_ref[...])


def _out(x, yg, lru, gates, wpa, wpb, wout, gffn, wup, wdown, gfin):
    batch, seq, _ = x.shape
    d_ff = wup.shape[1]
    tile = math.gcd(seq, OUT_TILE)
    const = lambda b, j: (0, 0)
    rows = lambda b, j: (b, j, 0)
    shifted = lambda b, j: (b, ROW_ALIGN * ((tile // ROW_ALIGN) * j + N_META // ROW_ALIGN), 0)
    single = pl.Buffered(1)
    vec = pl.BlockSpec((1, D_MODEL), const)
    mat = pl.BlockSpec((D_MODEL, D_MODEL), const, pipeline_mode=single)
    return pl.pallas_call(
        _out_kernel,
        grid=(batch, seq // tile),
        in_specs=[
            pl.BlockSpec((None, tile, D_MODEL), rows),
            pl.BlockSpec((pl.Element(1), pl.Element(tile), pl.Element(D_MODEL)), shifted),
            pl.BlockSpec((pl.Element(1), pl.Element(tile), pl.Element(D_MODEL)), shifted),
            pl.BlockSpec((pl.Element(1), pl.Element(tile), pl.Element(2 * D_MODEL)), shifted),
            mat, mat, mat, vec,
            pl.BlockSpec((D_MODEL, d_ff), const, pipeline_mode=single),
            pl.BlockSpec((d_ff, D_MODEL), const, pipeline_mode=single),
            vec,
        ],
        out_specs=pl.BlockSpec((None, tile, D_MODEL), rows),
        out_shape=jax.ShapeDtypeStruct((batch, seq, D_MODEL), F32),
        compiler_params=pltpu.CompilerParams(dimension_semantics=("arbitrary", "arbitrary"),
                                             vmem_limit_bytes=VMEM_LIMIT),
        name="outproj_mlp",
    )(x, yg, lru, gates, wpa, wpb, wout, gffn, wup, wdown, gfin)


def _block_diag_tiles(w):
    n, b, _ = w.shape
    per = MXU_TILE // b
    w = w.reshape(n // per, per, b, b)
    eye = jnp.eye(per, dtype=w.dtype)
    return (eye[None, :, None, :, None] * w[:, :, :, None, :]).reshape(n // per, MXU_TILE, MXU_TILE)


def kernel(x, meta_tokens, norm_mix_g, w_in, mu_shift, w0, w_decay_up, a0, w_aaa_up, w_gate_up, k_k, k_a, r_k,
           ln_x_w, ln_x_b, w_proj_a, conv_w, conv_b, lru_wa, lru_ba, lru_wx, lru_bx, lru_lambda, w_proj_b, w_out,
           norm_ffn_g, w_ff_up, w_ff_down, norm_final_g):
    batch, seq, d = x.shape
    assert d == D_MODEL and norm_mix_g.shape[0] == 1
    t_real = N_META + seq
    t_pad = -(-t_real // TIME_TILE) * TIME_TILE

    row = lambda p: p.reshape(1, -1).astype(F32)
    w_in_bf = w_in[0].astype(BF16)
    c_lru, c_gate = COLS_A, COLS_A + 2 * D_MODEL
    xs, lru, gates = _inproj(
        x, meta_tokens.astype(x.dtype), t_pad, row(norm_mix_g[0]), w_in_bf[:, :c_lru], w_in_bf[:, c_lru:c_gate],
        w_in_bf[:, c_gate:], row(mu_shift[0]), conv_w[0].astype(F32), row(conv_b[0]),
        _block_diag_tiles(lru_wa[0]).astype(BF16), row(lru_ba[0]), _block_diag_tiles(lru_wx[0]).astype(BF16),
        row(lru_bx[0]), row(lru_lambda[0]))

    ones_bd = _block_diag_tiles(jnp.ones((MXU_TILE // HEAD, HEAD, HEAD), BF16))[0]
    yg = _rwkv(xs, batch, t_pad, row(w0[0]), w_decay_up[0].astype(BF16), row(a0[0]),
               w_aaa_up[0].astype(BF16), w_gate_up[0].astype(BF16), row(k_k[0]), row(k_a[0]), row(r_k[0]),
               row(ln_x_w[0]), row(ln_x_b[0]), ones_bd)

    seq3 = lambda a: a.reshape(batch, t_pad, a.shape[-1])
    return _out(x, seq3(yg), seq3(lru), seq3(gates), w_proj_a[0].astype(BF16), w_proj_b[0].astype(BF16),
                w_out[0].astype(BF16), row(norm_ffn_g[0]), w_ff_up[0].astype(BF16), w_ff_down[0].astype(BF16),
                row(norm_final_g))
```

```python
import functools
import math

import jax
import jax.numpy as jnp
from jax import lax
from jax.experimental import pallas as pl
from jax.experimental.pallas import tpu as pltpu

F32 = jnp.float32
BF16 = jnp.bfloat16

D_MODEL = 1024
N_META = 16
HEAD = 64
N_HEADS = D_MODEL // HEAD
DECAY_LORA = 64
AAA_LORA = 64
GATE_LORA = 128
COLS_A = 3 * D_MODEL + DECAY_LORA + AAA_LORA + GATE_LORA
LN_X_EPS = 64e-5
RMS_EPS = 1e-6
LRU_C = 8.0
CONV_WIDTH = 4
EXP_M05 = math.exp(-0.5)
SQRT_2_OVER_PI = math.sqrt(2.0 / math.pi)

MXU_TILE = 256
SUBLANES = 8
CHUNK = 64
TIME_TILE = 192
LANES = 128
HEADS_PER_GROUP = 16
OUT_TILE = 512
ROW_ALIGN = 16
VMEM_LIMIT = 60 * 1024 * 1024

NT_DIMS = (((1,), (1,)), ((), ()))
TN_DIMS = (((0,), (0,)), ((), ()))


def _sigmoid(x):
    return 0.5 * jnp.tanh(0.5 * x) + 0.5


def _log1p(z):
    u = 1.0 + z
    return jnp.where(u == 1.0, z, jnp.log(u) * z / jnp.where(u == 1.0, 1.0, u - 1.0))


def _dot(a, b):
    return jnp.dot(a.astype(BF16), b.astype(BF16), preferred_element_type=F32)


def _dot_nt(a, b):
    return lax.dot_general(a.astype(BF16), b.astype(BF16), NT_DIMS, preferred_element_type=F32)


def _dot_tn(a, b):
    return lax.dot_general(a.astype(BF16), b.astype(BF16), TN_DIMS, preferred_element_type=F32)


def _dot_split(m, x):
    hi = x.astype(BF16)
    lo = (x - hi.astype(F32)).astype(BF16)
    return (jnp.dot(m, hi, preferred_element_type=F32) + jnp.dot(m, lo, preferred_element_type=F32))


def _rmsnorm(x, g):
    return x * lax.rsqrt(jnp.mean(x * x, axis=-1, keepdims=True) + RMS_EPS) * g


def _causal_conv(xb, cw_ref, cb_ref, xtail_ref):
    tt = xb.shape[0]
    tail = xtail_ref[...]
    row8 = lax.broadcasted_iota(jnp.int32, (SUBLANES, D_MODEL), 0)
    xc = cb_ref[...] + xb * cw_ref[CONV_WIDTH - 1:CONV_WIDTH, :]
    for s in range(1, CONV_WIDTH):
        sh = pltpu.roll(xb, s, 0)
        head = jnp.where(row8 < s, pltpu.roll(tail, s, 0), sh[0:SUBLANES])
        sh = jnp.concatenate([head, sh[SUBLANES:]], axis=0)
        xc = xc + sh * cw_ref[CONV_WIDTH - 1 - s:CONV_WIDTH - s, :]
    xtail_ref[...] = xb[tt - SUBLANES:tt]
    return xc


def _block_diag_dot(xb16, w_ref):
    return jnp.concatenate(
        [jnp.dot(xb16[:, j * MXU_TILE:(j + 1) * MXU_TILE], w_ref[j], preferred_element_type=F32)
         for j in range(D_MODEL // MXU_TILE)], axis=1)


def _rglru_recurrence(xc, yb, pre_r, pre_i, lam_ref, hcarry_ref, out_ref):
    tt = xc.shape[0]
    nl = -lam_ref[...]
    neg_c_softplus = -LRU_C * (jnp.maximum(nl, 0.0) + _log1p(jnp.exp(-jnp.abs(nl))))
    sub = lax.broadcasted_iota(jnp.int32, (ROW_ALIGN, D_MODEL), 0) & (SUBLANES - 1)
    carry = hcarry_ref[...]
    for r0 in range(0, tt, ROW_ALIGN):
        rows = slice(r0, r0 + ROW_ALIGN)
        log_a = neg_c_softplus * _sigmoid(pre_r[rows])
        a = jnp.exp(log_a)
        hs = jnp.sqrt(jnp.tanh(-log_a) * (a * a + 1.0)) * (_sigmoid(pre_i[rows]) * xc[rows])
        s = 1
        while s < SUBLANES:
            m = sub >= s
            hs = jnp.where(m, hs + a * pltpu.roll(hs, s, 0), hs)
            a = jnp.where(m, a * pltpu.roll(a, s, 0), a)
            s *= 2
        groups = []
        for g0 in range(0, ROW_ALIGN, SUBLANES):
            hg = hs[g0:g0 + SUBLANES] + a[g0:g0 + SUBLANES] * carry
            carry = hg[SUBLANES - 1:SUBLANES]
            groups.append(hg)
        y = yb[rows]
        gelu = 0.5 * y * (1.0 + jnp.tanh(SQRT_2_OVER_PI * (y + 0.044715 * (y * y * y))))
        out_ref[rows, :] = (jnp.concatenate(groups, axis=0) * gelu).astype(out_ref.dtype)
    hcarry_ref[...] = carry


def _inproj_kernel(tail_rows, x_ref, meta_ref, g_ref, wa_ref, wlru_ref, wgate_ref, mu_ref, cw_ref, cb_ref, lwa_ref,
                   ba_ref, lwx_ref, bx_ref, lam_ref, xs_ref, lru_ref, gates_ref, pcarry_ref, xtail_ref, hcarry_ref,
                   pa_ref):
    tt = x_ref.shape[1]

    @pl.when(pl.program_id(1) == 0)
    def _():
        pcarry_ref[...] = jnp.zeros_like(pcarry_ref)
        xtail_ref[...] = jnp.zeros_like(xtail_ref)
        hcarry_ref[...] = jnp.zeros_like(hcarry_ref)

    t = pl.program_id(1)
    blk = x_ref[0]
    first = jnp.concatenate([meta_ref[...], blk[:tt - N_META]], axis=0)
    if tail_rows < tt:
        last = jnp.concatenate([blk[tt - tail_rows:], jnp.zeros((tt - tail_rows, D_MODEL), F32)], axis=0)
    else:
        last = blk
    h = jnp.where(t == 0, first, jnp.where(t == pl.num_programs(1) - 1, last, blk))
    u = _rmsnorm(h, g_ref[...]).astype(BF16)

    pl_ = jnp.dot(u, wlru_ref[...], preferred_element_type=F32)
    xc = _causal_conv(pl_[:, 0:D_MODEL], cw_ref, cb_ref, xtail_ref)
    pa_ref[:, 0:D_MODEL] = jnp.dot(u, wa_ref[:, 0:D_MODEL], preferred_element_type=F32)
    xcb = xc.astype(BF16)
    pre_r = _block_diag_dot(xcb, lwa_ref) + ba_ref[...]
    pre_i = _block_diag_dot(xcb, lwx_ref) + bx_ref[...]
    pa_ref[:, D_MODEL:COLS_A] = jnp.dot(u, wa_ref[:, D_MODEL:COLS_A], preferred_element_type=F32)
    gates_ref[...] = jnp.dot(u, wgate_ref[...], preferred_element_type=F32)
    _rglru_recurrence(xc, pl_[:, D_MODEL:2 * D_MODEL], pre_r, pre_i, lam_ref, hcarry_ref, lru_ref)

    pa = pa_ref[...]
    row = lax.broadcasted_iota(jnp.int32, pa.shape, 0)
    prev = jnp.where(row == 0, pcarry_ref[...], pltpu.roll(pa, 1, 0))
    pcarry_ref[...] = pa[tt - 1:tt, :]
    xs_ref[...] = pa + (prev - pa) * mu_ref[...]


def _inproj(x, meta, t_pad, g, wa, wlru, wgate, mu, conv_w, conv_b, lwa, ba, lwx, bx, lam):
    batch, seq, _ = x.shape
    tt = TIME_TILE
    nt = t_pad // tt
    assert nt >= 2 and seq >= tt and seq % ROW_ALIGN == 0 and tt % ROW_ALIGN == 0 and N_META % ROW_ALIGN == 0
    tail_rows = N_META + seq - (nt - 1) * tt
    const = lambda b, t: (0, 0)
    rows = lambda b, t: (b * nt + t, 0)
    single = pl.Buffered(1)
    vec = pl.BlockSpec((1, D_MODEL), const)
    bd = pl.BlockSpec((D_MODEL // MXU_TILE, MXU_TILE, MXU_TILE), lambda b, t: (0, 0, 0))
    m = batch * t_pad
    return pl.pallas_call(
        functools.partial(_inproj_kernel, tail_rows),
        grid=(batch, nt),
        in_specs=[
            pl.BlockSpec((pl.Element(1), pl.Element(tt), pl.Element(D_MODEL)),
                         lambda b, t: (b, ROW_ALIGN * jnp.clip((tt // ROW_ALIGN) * t - N_META // ROW_ALIGN, 0,
                                                               (seq - tt) // ROW_ALIGN), 0)),
            pl.BlockSpec((N_META, D_MODEL), const),
            vec,
            pl.BlockSpec((D_MODEL, COLS_A), const, pipeline_mode=single),
            pl.BlockSpec((D_MODEL, 2 * D_MODEL), const, pipeline_mode=single),
            pl.BlockSpec((D_MODEL, 2 * D_MODEL), const, pipeline_mode=single),
            pl.BlockSpec((1, COLS_A), const),
            pl.BlockSpec((CONV_WIDTH, D_MODEL), const),
            vec, bd, vec, bd, vec, vec,
        ],
        out_specs=[
            pl.BlockSpec((tt, COLS_A), rows),
            pl.BlockSpec((tt, D_MODEL), rows),
            pl.BlockSpec((tt, 2 * D_MODEL), rows),
        ],
        out_shape=[jax.ShapeDtypeStruct((m, COLS_A), F32), jax.ShapeDtypeStruct((m, D_MODEL), BF16),
                   jax.ShapeDtypeStruct((m, 2 * D_MODEL), F32)],
        scratch_shapes=[
            pltpu.VMEM((1, COLS_A), F32),
            pltpu.VMEM((SUBLANES, D_MODEL), F32),
            pltpu.VMEM((1, D_MODEL), F32),
            pltpu.VMEM((tt, COLS_A), F32),
        ],
        compiler_params=pltpu.CompilerParams(dimension_semantics=("arbitrary", "arbitrary"),
                                             vmem_limit_bytes=VMEM_LIMIT),
        name="inproj_rglru",
    )(x, meta, g, wa, wlru, wgate, mu, conv_w, conv_b, lwa, ba, lwx, bx, lam)


def _wkv_terms(units):
    c = CHUNK
    f32dot = functools.partial(jnp.dot, preferred_element_type=F32)
    row = lax.broadcasted_iota(jnp.int32, (c, LANES), 0)
    lane = lax.broadcasted_iota(jnp.int32, (c, LANES), 1)
    low = lane < HEAD
    col = lane & (HEAD - 1)
    strict = col < row
    incl = col <= row
    eye_hi = jnp.where(lane - HEAD == row, 1.0, 0.0).astype(BF16)
    zeros = jnp.zeros((c, LANES), BF16)
    at, at_hi, bt, kt, rt, rt_hi, v, bh_hi, kh_hi = (list(x) for x in zip(*units))

    gq = [_dot_nt(jnp.concatenate([a, r], axis=0), jnp.concatenate([b, k], axis=0))
          for a, r, b, k in zip(at, rt, bt, kt)]
    g = [jnp.where(strict, x[:c], 0.0).astype(BF16) for x in gq]
    q = [jnp.where(incl, x[c:], 0.0).astype(BF16) for x in gq]
    x0 = [jnp.where(low, f32dot(y[:, HEAD:], vv).astype(BF16), a) for y, vv, a in zip(g, v, at_hi)]
    wb = [jnp.where(low, y, eye_hi) for y in g]
    n_fac = int(math.ceil(math.log2(c)))
    for i in range(n_fac):
        pw = [f32dot(x[:, :HEAD], x).astype(BF16) for x in wb]
        if i + 1 < n_fac:
            wb = [jnp.where(low, y, x + y) for x, y in zip(wb, pw)]
        else:
            wb = [x + y for x, y in zip(wb, pw)]
    tu = [f32dot(x[:, HEAD:], y).astype(BF16) for x, y in zip(wb, x0)]
    rhs2 = [jnp.concatenate([x, vv], axis=0) for x, vv in zip(tu, v)]
    ry = [f32dot(x, y) for x, y in zip(q, rhs2)]
    rp = [jnp.where(low, zeros, r + y.astype(BF16)) for r, y in zip(rt_hi, ry)]
    mc = [_dot_tn(y, jnp.concatenate([b, k], axis=0)) for y, b, k in zip(rhs2, bh_hi, kh_hi)]
    return rp, ry, mc


def _rwkv_kernel(xs_ref, w0_ref, wdu_ref, a0_ref, wau_ref, wgu_ref, kk_ref, ka_ref, rk_ref,
                 lnw_ref, lnb_ref, ones_ref, out_ref,
                 state_ref, at_s, ath_s, bt_s, kt_s, rt_s, rth_s, v_s, bh_s, kh_s, wl_s, y_s, yfull_ref):
    tt = xs_ref.shape[0]
    nc = tt // CHUNK

    @pl.when(pl.program_id(1) == 0)
    def _():
        state_ref[...] = jnp.zeros_like(state_ref)

    r = xs_ref[:, 0:D_MODEL]
    k = xs_ref[:, D_MODEL:2 * D_MODEL]
    v = xs_ref[:, 2 * D_MODEL:3 * D_MODEL]
    o = 3 * D_MODEL
    wd = xs_ref[:, o:o + DECAY_LORA]
    ad = xs_ref[:, o + DECAY_LORA:o + DECAY_LORA + AAA_LORA]
    gd = xs_ref[:, o + DECAY_LORA + AAA_LORA:COLS_A]

    lw = -EXP_M05 * _sigmoid(w0_ref[...] + _dot(jnp.tanh(wd), wdu_ref[...]))
    lr = _sigmoid(a0_ref[...] + _dot(ad, wau_ref[...]))
    g = _dot(_sigmoid(gd), wgu_ref[...])

    ones_bd = ones_ref[...]

    def head_sum(x):
        xb = x.astype(BF16)
        return jnp.concatenate(
            [jnp.dot(xb[:, j:j + MXU_TILE], ones_bd, preferred_element_type=F32)
             for j in range(0, D_MODEL, MXU_TILE)], axis=1)

    kk = k * kk_ref[...]
    kk = kk * lax.rsqrt(jnp.maximum(head_sum(kk * kk), 1e-24))
    kmod = k * (1.0 + (lr - 1.0) * ka_ref[...])
    bvec = kk * lr

    ri = lax.broadcasted_iota(jnp.int32, (tt, tt), 0)
    ci = lax.broadcasted_iota(jnp.int32, (tt, tt), 1)
    tri = jnp.where(((ri // CHUNK) == (ci // CHUNK)) & (ci <= ri), 1.0, 0.0).astype(BF16)
    cw = _dot_split(tri, lw)
    tot = jnp.concatenate([jnp.broadcast_to(cw[c0 + CHUNK - 1:c0 + CHUNK, :], (CHUNK, D_MODEL))
                           for c0 in range(0, tt, CHUNK)], axis=0)

    w_inc = jnp.exp(cw)
    w_inv = jnp.exp(-cw)
    w_prev = jnp.exp(cw - lw)
    w_end = jnp.exp(tot - cw)
    w_tot = jnp.exp(tot)

    at_f = -kk * w_prev
    bt_f = bvec * w_inv
    kt_f = kmod * w_inv
    rt_f = r * w_inc
    bh_f = bvec * w_end
    kh_f = kmod * w_end
    lane = lax.broadcasted_iota(jnp.int32, (tt, LANES), 1)
    low = lane < HEAD
    zero = jnp.zeros((tt, LANES), BF16)

    def pair_views(x_f, j, dtype=BF16):
        blk = x_f[:, j * LANES:(j + 1) * LANES].astype(dtype)
        return blk, pltpu.roll(blk, HEAD, 1)

    def lo_masked(blk, swapped):
        return jnp.where(low, blk, zero), jnp.where(low, swapped, zero)

    def hi_masked(blk, swapped):
        return jnp.where(low, zero, swapped), jnp.where(low, zero, blk)

    for j in range(N_HEADS // 2):
        at_b, at_sw = pair_views(at_f, j)
        rt_b, rt_sw = pair_views(rt_f, j)
        wl_b, wl_sw = pair_views(w_tot, j, F32)
        stores = ((at_s, (at_b, at_sw)), (ath_s, (at_sw, at_b)),
                  (bt_s, lo_masked(*pair_views(bt_f, j))), (kt_s, lo_masked(*pair_views(kt_f, j))),
                  (rt_s, (rt_b, rt_sw)), (rth_s, (rt_sw, rt_b)),
                  (v_s, lo_masked(*pair_views(v, j))),
                  (bh_s, hi_masked(*pair_views(bh_f, j))), (kh_s, hi_masked(*pair_views(kh_f, j))),
                  (wl_s, (wl_sw, wl_b)))
        for ref, (even, odd) in stores:
            ref[2 * j] = even
            ref[2 * j + 1] = odd

    for h0 in range(0, N_HEADS, HEADS_PER_GROUP):
        heads = range(h0, h0 + HEADS_PER_GROUP)
        units = []
        for h in heads:
            for c in range(nc):
                rows = pl.ds(c * CHUNK, CHUNK)
                units.append(tuple(ref[h, rows, :]
                                   for ref in (at_s, ath_s, bt_s, kt_s, rt_s, rth_s, v_s, bh_s, kh_s)))
        rp, ry, mc = _wkv_terms(units)
        s = [state_ref[h] for h in heads]
        for c in range(nc):
            rows = pl.ds(c * CHUNK, CHUNK)
            sb = [x.astype(BF16) for x in s]
            for j, h in enumerate(heads):
                u = j * nc + c
                y_s[h, rows, :] = _dot_nt(rp[u], sb[j]) + ry[u][:, :HEAD]
            s = [s[j] * wl_s[h, pl.ds(c * CHUNK, 1), :]
                 + jnp.dot(sb[j][:, HEAD:], mc[j * nc + c][CHUNK:].astype(BF16), preferred_element_type=F32)
                 + mc[j * nc + c][:CHUNK] for j, h in enumerate(heads)]
        for j, h in enumerate(heads):
            state_ref[h] = s[j]

    for h in range(N_HEADS):
        yfull_ref[:, h * HEAD:(h + 1) * HEAD] = y_s[h]
    y = yfull_ref[...]

    inv_n = 1.0 / HEAD
    mean = head_sum(y) * inv_n
    yc = y - mean
    var = head_sum(yc * yc) * inv_n
    yn = yc * lax.rsqrt(var + LN_X_EPS) * lnw_ref[...] + lnb_ref[...]
    bonus = head_sum(r * kmod * rk_ref[...]) * v
    out_ref[...] = ((yn + bonus) * g).astype(out_ref.dtype)


def _rwkv(xs, batch, t_pad, w0, wdu, a0, wau, wgu, k_k, k_a, r_k, ln_w, ln_b, ones_bd):
    tt = TIME_TILE
    nt = t_pad // tt
    const = lambda b, t: (0, 0)
    vec = pl.BlockSpec((1, D_MODEL), const)
    head_bf = pltpu.VMEM((N_HEADS, tt, LANES), BF16)
    return pl.pallas_call(
        _rwkv_kernel,
        grid=(batch, nt),
        in_specs=[
            pl.BlockSpec((tt, COLS_A), lambda b, t: (b * nt + t, 0)),
            vec,
            pl.BlockSpec((DECAY_LORA, D_MODEL), const),
            vec,
            pl.BlockSpec((AAA_LORA, D_MODEL), const),
            pl.BlockSpec((GATE_LORA, D_MODEL), const),
            vec, vec, vec, vec, vec,
            pl.BlockSpec((MXU_TILE, MXU_TILE), const),
        ],
        out_specs=pl.BlockSpec((tt, D_MODEL), lambda b, t: (b * nt + t, 0)),
        out_shape=jax.ShapeDtypeStruct((batch * t_pad, D_MODEL), BF16),
        scratch_shapes=[
            pltpu.VMEM((N_HEADS, HEAD, LANES), F32),
            head_bf, head_bf, head_bf, head_bf, head_bf, head_bf, head_bf, head_bf, head_bf,
            pltpu.VMEM((N_HEADS, tt, LANES), F32),
            pltpu.VMEM((N_HEADS, tt, HEAD), F32),
            pltpu.VMEM((tt, D_MODEL), F32),
        ],
        compiler_params=pltpu.CompilerParams(dimension_semantics=("arbitrary", "arbitrary"),
                                             vmem_limit_bytes=VMEM_LIMIT),
        name="rwkv7_mix",
    )(xs, w0, wdu, a0, wau, wgu, k_k, k_a, r_k, ln_w, ln_b, ones_bd)


def _out_kernel(x_ref, yg_ref, lru_ref, gates_ref, wpa_ref, wpb_ref, wout_ref, gffn_ref, wup_ref, wdown_ref,
                gfin_ref, out_ref):
    ya = jnp.dot(yg_ref[0], wpa_ref[...], preferred_element_type=F32)
    yr = jnp.dot(lru_ref[0], wpb_ref[...], preferred_element_type=F32)
    ga = _sigmoid(gates_ref[0, :, 0:D_MODEL])
    gb = _sigmoid(gates_ref[0, :, D_MODEL:2 * D_MODEL])
    h1 = x_ref[...] + _dot(ga * ya + gb * yr, wout_ref[...])
    u = _rmsnorm(h1, gffn_ref[...]).astype(BF16)
    h2 = h1
    for c0 in range(0, wup_ref.shape[1], D_MODEL):
        z = jnp.dot(u, wup_ref[:, c0:c0 + D_MODEL], preferred_element_type=F32)
        z = jnp.square(jnp.maximum(z, 0.0))
        h2 = h2 + _dot(z, wdown_ref[c0:c0 + D_MODEL, :])
    out_ref[...] = _rmsnorm(h2, gfin_ref[...])


def _out(x, yg, lru, gates, wpa, wpb, wout, gffn, wup, wdown, gfin):
    batch, seq, _ = x.shape
    d_ff = wup.shape[1]
    tile = math.gcd(seq, OUT_TILE)
    const = lambda b, j: (0, 0)
    rows = lambda b, j: (b, j, 0)
    shifted = lambda b, j: (b, ROW_ALIGN * ((tile // ROW_ALIGN) * j + N_META // ROW_ALIGN), 0)
    single = pl.Buffered(1)
    vec = pl.BlockSpec((1, D_MODEL), const)
    mat = pl.BlockSpec((D_MODEL, D_MODEL), const, pipeline_mode=single)
    return pl.pallas_call(
        _out_kernel,
        grid=(batch, seq // tile),
        in_specs=[
            pl.BlockSpec((None, tile, D_MODEL), rows),
            pl.BlockSpec((pl.Element(1), pl.Element(tile), pl.Element(D_MODEL)), shifted),
            pl.BlockSpec((pl.Element(1), pl.Element(tile), pl.Element(D_MODEL)), shifted),
            pl.BlockSpec((pl.Element(1), pl.Element(tile), pl.Element(2 * D_MODEL)), shifted),
            mat, mat, mat, vec,
            pl.BlockSpec((D_MODEL, d_ff), const, pipeline_mode=single),
            pl.BlockSpec((d_ff, D_MODEL), const, pipeline_mode=single),
            vec,
        ],
        out_specs=pl.BlockSpec((None, tile, D_MODEL), rows),
        out_shape=jax.ShapeDtypeStruct((batch, seq, D_MODEL), F32),
        compiler_params=pltpu.CompilerParams(dimension_semantics=("arbitrary", "arbitrary"),
                                             vmem_limit_bytes=VMEM_LIMIT),
        name="outproj_mlp",
    )(x, yg, lru, gates, wpa, wpb, wout, gffn, wup, wdown, gfin)


def _block_diag_tiles(w):
    n, b, _ = w.shape
    per = MXU_TILE // b
    w = w.reshape(n // per, per, b, b)
    eye = jnp.eye(per, dtype=w.dtype)
    return (eye[None, :, None, :, None] * w[:, :, :, None, :]).reshape(n // per, MXU_TILE, MXU_TILE)


def kernel(x, meta_tokens, norm_mix_g, w_in, mu_shift, w0, w_decay_up, a0, w_aaa_up, w_gate_up, k_k, k_a, r_k,
           ln_x_w, ln_x_b, w_proj_a, conv_w, conv_b, lru_wa, lru_ba, lru_wx, lru_bx, lru_lambda, w_proj_b, w_out,
           norm_ffn_g, w_ff_up, w_ff_down, norm_final_g):
    batch, seq, d = x.shape
    assert d == D_MODEL and norm_mix_g.shape[0] == 1
    t_real = N_META + seq
    t_pad = -(-t_real // TIME_TILE) * TIME_TILE

    row = lambda p: p.reshape(1, -1).astype(F32)
    w_in_bf = w_in[0].astype(BF16)
    c_lru, c_gate = COLS_A, COLS_A + 2 * D_MODEL
    xs, lru, gates = _inproj(
        x, meta_tokens.astype(x.dtype), t_pad, row(norm_mix_g[0]), w_in_bf[:, :c_lru], w_in_bf[:, c_lru:c_gate],
        w_in_bf[:, c_gate:], row(mu_shift[0]), conv_w[0].astype(F32), row(conv_b[0]),
        _block_diag_tiles(lru_wa[0]).astype(BF16), row(lru_ba[0]), _block_diag_tiles(lru_wx[0]).astype(BF16),
        row(lru_bx[0]), row(lru_lambda[0]))

    ones_bd = _block_diag_tiles(jnp.ones((MXU_TILE // HEAD, HEAD, HEAD), BF16))[0]
    yg = _rwkv(xs, batch, t_pad, row(w0[0]), w_decay_up[0].astype(BF16), row(a0[0]),
               w_aaa_up[0].astype(BF16), w_gate_up[0].astype(BF16), row(k_k[0]), row(k_a[0]), row(r_k[0]),
               row(ln_x_w[0]), row(ln_x_b[0]), ones_bd)

    seq3 = lambda a: a.reshape(batch, t_pad, a.shape[-1])
    return _out(x, seq3(yg), seq3(lru), seq3(gates), w_proj_a[0].astype(BF16), w_proj_b[0].astype(BF16),
                w_out[0].astype(BF16), row(norm_ffn_g[0]), w_ff_up[0].astype(BF16), w_ff_down[0].astype(BF16),
                row(norm_final_g))
```

```python
import functools
import math

import jax
import jax.numpy as jnp
from jax import lax
from jax.experimental import pallas as pl
from jax.experimental.pallas import tpu as pltpu

F32 = jnp.float32
BF16 = jnp.bfloat16

D_MODEL = 1024
N_META = 16
HEAD = 64
N_HEADS = D_MODEL // HEAD
DECAY_LORA = 64
AAA_LORA = 64
GATE_LORA = 128
COLS_A = 3 * D_MODEL + DECAY_LORA + AAA_LORA + GATE_LORA
LN_X_EPS = 64e-5
RMS_EPS = 1e-6
LRU_C = 8.0
CONV_WIDTH = 4
EXP_M05 = math.exp(-0.5)
SQRT_2_OVER_PI = math.sqrt(2.0 / math.pi)

MXU_TILE = 256
SUBLANES = 8
CHUNK = 64
TIME_TILE = 192
LANES = 128
HEADS_PER_GROUP = 16
OUT_TILE = 512
ROW_ALIGN = 16
VMEM_LIMIT = 60 * 1024 * 1024

NT_DIMS = (((1,), (1,)), ((), ()))
TN_DIMS = (((0,), (0,)), ((), ()))


def _sigmoid(x):
    return 0.5 * jnp.tanh(0.5 * x) + 0.5


def _log1p(z):
    u = 1.0 + z
    return jnp.where(u == 1.0, z, jnp.log(u) * z / jnp.where(u == 1.0, 1.0, u - 1.0))


def _dot(a, b):
    return jnp.dot(a.astype(BF16), b.astype(BF16), preferred_element_type=F32)


def _dot_nt(a, b):
    return lax.dot_general(a.astype(BF16), b.astype(BF16), NT_DIMS, preferred_element_type=F32)


def _dot_tn(a, b):
    return lax.dot_general(a.astype(BF16), b.astype(BF16), TN_DIMS, preferred_element_type=F32)


def _dot_split(m, x):
    hi = x.astype(BF16)
    lo = (x - hi.astype(F32)).astype(BF16)
    return (jnp.dot(m, hi, preferred_element_type=F32) + jnp.dot(m, lo, preferred_element_type=F32))


def _rmsnorm(x, g):
    return x * lax.rsqrt(jnp.mean(x * x, axis=-1, keepdims=True) + RMS_EPS) * g


def _causal_conv(xb, cw_ref, cb_ref, xtail_ref):
    tt = xb.shape[0]
    tail = xtail_ref[...]
    row8 = lax.broadcasted_iota(jnp.int32, (SUBLANES, D_MODEL), 0)
    xc = cb_ref[...] + xb * cw_ref[CONV_WIDTH - 1:CONV_WIDTH, :]
    for s in range(1, CONV_WIDTH):
        sh = pltpu.roll(xb, s, 0)
        head = jnp.where(row8 < s, pltpu.roll(tail, s, 0), sh[0:SUBLANES])
        sh = jnp.concatenate([head, sh[SUBLANES:]], axis=0)
        xc = xc + sh * cw_ref[CONV_WIDTH - 1 - s:CONV_WIDTH - s, :]
    xtail_ref[...] = xb[tt - SUBLANES:tt]
    return xc


def _block_diag_dot(xb16, w_ref):
    return jnp.concatenate(
        [jnp.dot(xb16[:, j * MXU_TILE:(j + 1) * MXU_TILE], w_ref[j], preferred_element_type=F32)
         for j in range(D_MODEL // MXU_TILE)], axis=1)


def _gelu(y):
    return 0.5 * y * (1.0 + jnp.tanh(SQRT_2_OVER_PI * (y + 0.044715 * (y * y * y))))


def _rglru_recurrence(xc, pre_r, pre_i, lam_ref, hcarry_ref, out_ref):
    tt = xc.shape[0]
    nl = -lam_ref[...]
    neg_c_softplus = -LRU_C * (jnp.maximum(nl, 0.0) + _log1p(jnp.exp(-jnp.abs(nl))))
    sub = lax.broadcasted_iota(jnp.int32, (ROW_ALIGN, D_MODEL), 0) & (SUBLANES - 1)
    carry = hcarry_ref[...]
    for r0 in range(0, tt, ROW_ALIGN):
        rows = slice(r0, r0 + ROW_ALIGN)
        log_a = neg_c_softplus * _sigmoid(pre_r[rows])
        a = jnp.exp(log_a)
        hs = jnp.sqrt(jnp.tanh(-log_a) * (a * a + 1.0)) * (_sigmoid(pre_i[rows]) * xc[rows])
        s = 1
        while s < SUBLANES:
            m = sub >= s
            hs = jnp.where(m, hs + a * pltpu.roll(hs, s, 0), hs)
            a = jnp.where(m, a * pltpu.roll(a, s, 0), a)
            s *= 2
        groups = []
        for g0 in range(0, ROW_ALIGN, SUBLANES):
            hg = hs[g0:g0 + SUBLANES] + a[g0:g0 + SUBLANES] * carry
            carry = hg[SUBLANES - 1:SUBLANES]
            groups.append(hg)
        out_ref[rows, :] = jnp.concatenate(groups, axis=0).astype(out_ref.dtype)
    hcarry_ref[...] = carry


def _inproj_kernel(tail_rows, x_ref, meta_ref, g_ref, wa_ref, wlru_ref, wgate_ref, mu_ref, cw_ref, cb_ref, lwa_ref,
                   ba_ref, lwx_ref, bx_ref, lam_ref, xs_ref, lru_ref, yb_ref, gates_ref, pcarry_ref, xtail_ref,
                   hcarry_ref, pa_ref):
    tt = x_ref.shape[1]

    @pl.when(pl.program_id(1) == 0)
    def _():
        pcarry_ref[...] = jnp.zeros_like(pcarry_ref)
        xtail_ref[...] = jnp.zeros_like(xtail_ref)
        hcarry_ref[...] = jnp.zeros_like(hcarry_ref)

    t = pl.program_id(1)
    blk = x_ref[0]
    first = jnp.concatenate([meta_ref[...], blk[:tt - N_META]], axis=0)
    if tail_rows < tt:
        last = jnp.concatenate([blk[tt - tail_rows:], jnp.zeros((tt - tail_rows, D_MODEL), F32)], axis=0)
    else:
        last = blk
    h = jnp.where(t == 0, first, jnp.where(t == pl.num_programs(1) - 1, last, blk))
    u = _rmsnorm(h, g_ref[...]).astype(BF16)

    pl_ = jnp.dot(u, wlru_ref[...], preferred_element_type=F32)
    xc = _causal_conv(pl_[:, 0:D_MODEL], cw_ref, cb_ref, xtail_ref)
    pa_ref[:, 0:D_MODEL] = jnp.dot(u, wa_ref[:, 0:D_MODEL], preferred_element_type=F32)
    xcb = xc.astype(BF16)
    pre_r = _block_diag_dot(xcb, lwa_ref) + ba_ref[...]
    pre_i = _block_diag_dot(xcb, lwx_ref) + bx_ref[...]
    pa_ref[:, D_MODEL:COLS_A] = jnp.dot(u, wa_ref[:, D_MODEL:COLS_A], preferred_element_type=F32)
    gates_ref[...] = jnp.dot(u, wgate_ref[...], preferred_element_type=F32)
    yb_ref[...] = pl_[:, D_MODEL:2 * D_MODEL]
    _rglru_recurrence(xc, pre_r, pre_i, lam_ref, hcarry_ref, lru_ref)

    pa = pa_ref[...]
    row = lax.broadcasted_iota(jnp.int32, pa.shape, 0)
    prev = jnp.where(row == 0, pcarry_ref[...], pltpu.roll(pa, 1, 0))
    pcarry_ref[...] = pa[tt - 1:tt, :]
    xs_ref[...] = pa + (prev - pa) * mu_ref[...]


def _inproj(x, meta, t_pad, g, wa, wlru, wgate, mu, conv_w, conv_b, lwa, ba, lwx, bx, lam):
    batch, seq, _ = x.shape
    tt = TIME_TILE
    nt = t_pad // tt
    assert nt >= 2 and seq >= tt and seq % ROW_ALIGN == 0 and tt % ROW_ALIGN == 0 and N_META % ROW_ALIGN == 0
    tail_rows = N_META + seq - (nt - 1) * tt
    const = lambda b, t: (0, 0)
    rows = lambda b, t: (b * nt + t, 0)
    single = pl.Buffered(1)
    vec = pl.BlockSpec((1, D_MODEL), const)
    bd = pl.BlockSpec((D_MODEL // MXU_TILE, MXU_TILE, MXU_TILE), lambda b, t: (0, 0, 0))
    m = batch * t_pad
    return pl.pallas_call(
        functools.partial(_inproj_kernel, tail_rows),
        grid=(batch, nt),
        in_specs=[
            pl.BlockSpec((pl.Element(1), pl.Element(tt), pl.Element(D_MODEL)),
                         lambda b, t: (b, ROW_ALIGN * jnp.clip((tt // ROW_ALIGN) * t - N_META // ROW_ALIGN, 0,
                                                               (seq - tt) // ROW_ALIGN), 0)),
            pl.BlockSpec((N_META, D_MODEL), const),
            vec,
            pl.BlockSpec((D_MODEL, COLS_A), const, pipeline_mode=single),
            pl.BlockSpec((D_MODEL, 2 * D_MODEL), const, pipeline_mode=single),
            pl.BlockSpec((D_MODEL, 2 * D_MODEL), const, pipeline_mode=single),
            pl.BlockSpec((1, COLS_A), const),
            pl.BlockSpec((CONV_WIDTH, D_MODEL), const),
            vec, bd, vec, bd, vec, vec,
        ],
        out_specs=[
            pl.BlockSpec((tt, COLS_A), rows),
            pl.BlockSpec((tt, D_MODEL), rows),
            pl.BlockSpec((tt, D_MODEL), rows),
            pl.BlockSpec((tt, 2 * D_MODEL), rows),
        ],
        out_shape=[jax.ShapeDtypeStruct((m, COLS_A), F32), jax.ShapeDtypeStruct((m, D_MODEL), BF16),
                   jax.ShapeDtypeStruct((m, D_MODEL), F32), jax.ShapeDtypeStruct((m, 2 * D_MODEL), F32)],
        scratch_shapes=[
            pltpu.VMEM((1, COLS_A), F32),
            pltpu.VMEM((SUBLANES, D_MODEL), F32),
            pltpu.VMEM((1, D_MODEL), F32),
            pltpu.VMEM((tt, COLS_A), F32),
        ],
        compiler_params=pltpu.CompilerParams(dimension_semantics=("arbitrary", "arbitrary"),
                                             vmem_limit_bytes=VMEM_LIMIT),
        name="inproj_rglru",
    )(x, meta, g, wa, wlru, wgate, mu, conv_w, conv_b, lwa, ba, lwx, bx, lam)


def _wkv_terms(units):
    c = CHUNK
    f32dot = functools.partial(jnp.dot, preferred_element_type=F32)
    row = lax.broadcasted_iota(jnp.int32, (c, LANES), 0)
    lane = lax.broadcasted_iota(jnp.int32, (c, LANES), 1)
    low = lane < HEAD
    col = lane & (HEAD - 1)
    strict = col < row
    incl = col <= row
    eye_hi = jnp.where(lane - HEAD == row, 1.0, 0.0).astype(BF16)
    zeros = jnp.zeros((c, LANES), BF16)
    at, at_hi, bt, kt, rt, rt_hi, v, bh_hi, kh_hi = (list(x) for x in zip(*units))

    gq = [_dot_nt(jnp.concatenate([a, r], axis=0), jnp.concatenate([b, k], axis=0))
          for a, r, b, k in zip(at, rt, bt, kt)]
    g = [jnp.where(strict, x[:c], 0.0).astype(BF16) for x in gq]
    q = [jnp.where(incl, x[c:], 0.0).astype(BF16) for x in gq]
    x0 = [jnp.where(low, f32dot(y[:, HEAD:], vv).astype(BF16), a) for y, vv, a in zip(g, v, at_hi)]
    wb = [jnp.where(low, y, eye_hi) for y in g]
    n_fac = int(math.ceil(math.log2(c)))
    for i in range(n_fac):
        pw = [f32dot(x[:, :HEAD], x).astype(BF16) for x in wb]
        if i + 1 < n_fac:
            wb = [jnp.where(low, y, x + y) for x, y in zip(wb, pw)]
        else:
            wb = [x + y for x, y in zip(wb, pw)]
    tu = [f32dot(x[:, HEAD:], y).astype(BF16) for x, y in zip(wb, x0)]
    rhs2 = [jnp.concatenate([x, vv], axis=0) for x, vv in zip(tu, v)]
    ry = [f32dot(x, y) for x, y in zip(q, rhs2)]
    rp = [jnp.where(low, zeros, r + y.astype(BF16)) for r, y in zip(rt_hi, ry)]
    mc = [_dot_tn(y, jnp.concatenate([b, k], axis=0)) for y, b, k in zip(rhs2, bh_hi, kh_hi)]
    return rp, ry, mc


def _rwkv_kernel(xs_ref, w0_ref, wdu_ref, a0_ref, wau_ref, wgu_ref, kk_ref, ka_ref, rk_ref,
                 lnw_ref, lnb_ref, ones_ref, out_ref,
                 state_ref, at_s, ath_s, bt_s, kt_s, rt_s, rth_s, v_s, bh_s, kh_s, wl_s, y_s, yfull_ref):
    tt = xs_ref.shape[0]
    nc = tt // CHUNK

    @pl.when(pl.program_id(1) == 0)
    def _():
        state_ref[...] = jnp.zeros_like(state_ref)

    r = xs_ref[:, 0:D_MODEL]
    k = xs_ref[:, D_MODEL:2 * D_MODEL]
    v = xs_ref[:, 2 * D_MODEL:3 * D_MODEL]
    o = 3 * D_MODEL
    wd = xs_ref[:, o:o + DECAY_LORA]
    ad = xs_ref[:, o + DECAY_LORA:o + DECAY_LORA + AAA_LORA]
    gd = xs_ref[:, o + DECAY_LORA + AAA_LORA:COLS_A]

    lw = -EXP_M05 * _sigmoid(w0_ref[...] + _dot(jnp.tanh(wd), wdu_ref[...]))
    lr = _sigmoid(a0_ref[...] + _dot(ad, wau_ref[...]))
    g = _dot(_sigmoid(gd), wgu_ref[...])

    ones_bd = ones_ref[...]

    def head_sum(x):
        xb = x.astype(BF16)
        return jnp.concatenate(
            [jnp.dot(xb[:, j:j + MXU_TILE], ones_bd, preferred_element_type=F32)
             for j in range(0, D_MODEL, MXU_TILE)], axis=1)

    kk = k * kk_ref[...]
    kk = kk * lax.rsqrt(jnp.maximum(head_sum(kk * kk), 1e-24))
    kmod = k * (1.0 + (lr - 1.0) * ka_ref[...])
    bvec = kk * lr

    ri = lax.broadcasted_iota(jnp.int32, (tt, tt), 0)
    ci = lax.broadcasted_iota(jnp.int32, (tt, tt), 1)
    tri = jnp.where(((ri // CHUNK) == (ci // CHUNK)) & (ci <= ri), 1.0, 0.0).astype(BF16)
    cw = _dot_split(tri, lw)
    tot = jnp.concatenate([jnp.broadcast_to(cw[c0 + CHUNK - 1:c0 + CHUNK, :], (CHUNK, D_MODEL))
                           for c0 in range(0, tt, CHUNK)], axis=0)

    w_inc = jnp.exp(cw)
    w_inv = jnp.exp(-cw)
    w_prev = jnp.exp(cw - lw)
    w_end = jnp.exp(tot - cw)
    w_tot = jnp.exp(tot)

    at_f = -kk * w_prev
    bt_f = bvec * w_inv
    kt_f = kmod * w_inv
    rt_f = r * w_inc
    bh_f = bvec * w_end
    kh_f = kmod * w_end
    lane = lax.broadcasted_iota(jnp.int32, (tt, LANES), 1)
    low = lane < HEAD
    zero = jnp.zeros((tt, LANES), BF16)

    def pair_views(x_f, j, dtype=BF16):
        blk = x_f[:, j * LANES:(j + 1) * LANES].astype(dtype)
        return blk, pltpu.roll(blk, HEAD, 1)

    def lo_masked(blk, swapped):
        return jnp.where(low, blk, zero), jnp.where(low, swapped, zero)

    def hi_masked(blk, swapped):
        return jnp.where(low, zero, swapped), jnp.where(low, zero, blk)

    for j in range(N_HEADS // 2):
        at_b, at_sw = pair_views(at_f, j)
        rt_b, rt_sw = pair_views(rt_f, j)
        wl_b, wl_sw = pair_views(w_tot, j, F32)
        stores = ((at_s, (at_b, at_sw)), (ath_s, (at_sw, at_b)),
                  (bt_s, lo_masked(*pair_views(bt_f, j))), (kt_s, lo_masked(*pair_views(kt_f, j))),
                  (rt_s, (rt_b, rt_sw)), (rth_s, (rt_sw, rt_b)),
                  (v_s, lo_masked(*pair_views(v, j))),
                  (bh_s, hi_masked(*pair_views(bh_f, j))), (kh_s, hi_masked(*pair_views(kh_f, j))),
                  (wl_s, (wl_sw, wl_b)))
        for ref, (even, odd) in stores:
            ref[2 * j] = even
            ref[2 * j + 1] = odd

    for h0 in range(0, N_HEADS, HEADS_PER_GROUP):
        heads = range(h0, h0 + HEADS_PER_GROUP)
        units = []
        for h in heads:
            for c in range(nc):
                rows = pl.ds(c * CHUNK, CHUNK)
                units.append(tuple(ref[h, rows, :]
                                   for ref in (at_s, ath_s, bt_s, kt_s, rt_s, rth_s, v_s, bh_s, kh_s)))
        rp, ry, mc = _wkv_terms(units)
        s = [state_ref[h] for h in heads]
        for c in range(nc):
            rows = pl.ds(c * CHUNK, CHUNK)
            sb = [x.astype(BF16) for x in s]
            for j, h in enumerate(heads):
                u = j * nc + c
                y_s[h, rows, :] = _dot_nt(rp[u], sb[j]) + ry[u][:, :HEAD]
            s = [s[j] * wl_s[h, pl.ds(c * CHUNK, 1), :]
                 + jnp.dot(sb[j][:, HEAD:], mc[j * nc + c][CHUNK:].astype(BF16), preferred_element_type=F32)
                 + mc[j * nc + c][:CHUNK] for j, h in enumerate(heads)]
        for j, h in enumerate(heads):
            state_ref[h] = s[j]

    for h in range(N_HEADS):
        yfull_ref[:, h * HEAD:(h + 1) * HEAD] = y_s[h]
    y = yfull_ref[...]

    inv_n = 1.0 / HEAD
    mean = head_sum(y) * inv_n
    yc = y - mean
    var = head_sum(yc * yc) * inv_n
    yn = yc * lax.rsqrt(var + LN_X_EPS) * lnw_ref[...] + lnb_ref[...]
    bonus = head_sum(r * kmod * rk_ref[...]) * v
    out_ref[...] = ((yn + bonus) * g).astype(out_ref.dtype)


def _rwkv(xs, batch, t_pad, w0, wdu, a0, wau, wgu, k_k, k_a, r_k, ln_w, ln_b, ones_bd):
    tt = TIME_TILE
    nt = t_pad // tt
    const = lambda b, t: (0, 0)
    vec = pl.BlockSpec((1, D_MODEL), const)
    head_bf = pltpu.VMEM((N_HEADS, tt, LANES), BF16)
    return pl.pallas_call(
        _rwkv_kernel,
        grid=(batch, nt),
        in_specs=[
            pl.BlockSpec((tt, COLS_A), lambda b, t: (b * nt + t, 0)),
            vec,
            pl.BlockSpec((DECAY_LORA, D_MODEL), const),
            vec,
            pl.BlockSpec((AAA_LORA, D_MODEL), const),
            pl.BlockSpec((GATE_LORA, D_MODEL), const),
            vec, vec, vec, vec, vec,
            pl.BlockSpec((MXU_TILE, MXU_TILE), const),
        ],
        out_specs=pl.BlockSpec((tt, D_MODEL), lambda b, t: (b * nt + t, 0)),
        out_shape=jax.ShapeDtypeStruct((batch * t_pad, D_MODEL), BF16),
        scratch_shapes=[
            pltpu.VMEM((N_HEADS, HEAD, LANES), F32),
            head_bf, head_bf, head_bf, head_bf, head_bf, head_bf, head_bf, head_bf, head_bf,
            pltpu.VMEM((N_HEADS, tt, LANES), F32),
            pltpu.VMEM((N_HEADS, tt, HEAD), F32),
            pltpu.VMEM((tt, D_MODEL), F32),
        ],
        compiler_params=pltpu.CompilerParams(dimension_semantics=("arbitrary", "arbitrary"),
                                             vmem_limit_bytes=VMEM_LIMIT),
        name="rwkv7_mix",
    )(xs, w0, wdu, a0, wau, wgu, k_k, k_a, r_k, ln_w, ln_b, ones_bd)


def _out_kernel(x_ref, yg_ref, lru_ref, yb_ref, gates_ref, wpa_ref, wpb_ref, wout_ref, gffn_ref, wup_ref, wdown_ref,
                gfin_ref, out_ref):
    ya = jnp.dot(yg_ref[0], wpa_ref[...], preferred_element_type=F32)
    yr = _dot(lru_ref[0].astype(F32) * _gelu(yb_ref[0]), wpb_ref[...])
    ga = _sigmoid(gates_ref[0, :, 0:D_MODEL])
    gb = _sigmoid(gates_ref[0, :, D_MODEL:2 * D_MODEL])
    h1 = x_ref[...] + _dot(ga * ya + gb * yr, wout_ref[...])
    u = _rmsnorm(h1, gffn_ref[...]).astype(BF16)
    h2 = h1
    for c0 in range(0, wup_ref.shape[1], D_MODEL):
        z = jnp.dot(u, wup_ref[:, c0:c0 + D_MODEL], preferred_element_type=F32)
        z = jnp.square(jnp.maximum(z, 0.0))
        h2 = h2 + _dot(z, wdown_ref[c0:c0 + D_MODEL, :])
    out_ref[...] = _rmsnorm(h2, gfin_ref[...])


def _out(x, yg, lru, yb, gates, wpa, wpb, wout, gffn, wup, wdown, gfin):
    batch, seq, _ = x.shape
    d_ff = wup.shape[1]
    tile = math.gcd(seq, OUT_TILE)
    const = lambda b, j: (0, 0)
    rows = lambda b, j: (b, j, 0)
    shifted = lambda b, j: (b, ROW_ALIGN * ((tile // ROW_ALIGN) * j + N_META // ROW_ALIGN), 0)
    single = pl.Buffered(1)
    vec = pl.BlockSpec((1, D_MODEL), const)
    mat = pl.BlockSpec((D_MODEL, D_MODEL), const, pipeline_mode=single)
    return pl.pallas_call(
        _out_kernel,
        grid=(batch, seq // tile),
        in_specs=[
            pl.BlockSpec((None, tile, D_MODEL), rows),
            pl.BlockSpec((pl.Element(1), pl.Element(tile), pl.Element(D_MODEL)), shifted),
            pl.BlockSpec((pl.Element(1), pl.Element(tile), pl.Element(D_MODEL)), shifted),
            pl.BlockSpec((pl.Element(1), pl.Element(tile), pl.Element(D_MODEL)), shifted),
            pl.BlockSpec((pl.Element(1), pl.Element(tile), pl.Element(2 * D_MODEL)), shifted),
            mat, mat, mat, vec,
            pl.BlockSpec((D_MODEL, d_ff), const, pipeline_mode=single),
            pl.BlockSpec((d_ff, D_MODEL), const, pipeline_mode=single),
            vec,
        ],
        out_specs=pl.BlockSpec((None, tile, D_MODEL), rows),
        out_shape=jax.ShapeDtypeStruct((batch, seq, D_MODEL), F32),
        compiler_params=pltpu.CompilerParams(dimension_semantics=("arbitrary", "arbitrary"),
                                             vmem_limit_bytes=VMEM_LIMIT),
        name="outproj_mlp",
    )(x, yg, lru, yb, gates, wpa, wpb, wout, gffn, wup, wdown, gfin)


def _block_diag_tiles(w):
    n, b, _ = w.shape
    per = MXU_TILE // b
    w = w.reshape(n // per, per, b, b)
    eye = jnp.eye(per, dtype=w.dtype)
    return (eye[None, :, None, :, None] * w[:, :, :, None, :]).reshape(n // per, MXU_TILE, MXU_TILE)


def kernel(x, meta_tokens, norm_mix_g, w_in, mu_shift, w0, w_decay_up, a0, w_aaa_up, w_gate_up, k_k, k_a, r_k,
           ln_x_w, ln_x_b, w_proj_a, conv_w, conv_b, lru_wa, lru_ba, lru_wx, lru_bx, lru_lambda, w_proj_b, w_out,
           norm_ffn_g, w_ff_up, w_ff_down, norm_final_g):
    batch, seq, d = x.shape
    assert d == D_MODEL and norm_mix_g.shape[0] == 1
    t_real = N_META + seq
    t_pad = -(-t_real // TIME_TILE) * TIME_TILE

    row = lambda p: p.reshape(1, -1).astype(F32)
    w_in_bf = w_in[0].astype(BF16)
    c_lru, c_gate = COLS_A, COLS_A + 2 * D_MODEL
    xs, lru, yb, gates = _inproj(
        x, meta_tokens.astype(x.dtype), t_pad, row(norm_mix_g[0]), w_in_bf[:, :c_lru], w_in_bf[:, c_lru:c_gate],
        w_in_bf[:, c_gate:], row(mu_shift[0]), conv_w[0].astype(F32), row(conv_b[0]),
        _block_diag_tiles(lru_wa[0]).astype(BF16), row(lru_ba[0]), _block_diag_tiles(lru_wx[0]).astype(BF16),
        row(lru_bx[0]), row(lru_lambda[0]))

    ones_bd = _block_diag_tiles(jnp.ones((MXU_TILE // HEAD, HEAD, HEAD), BF16))[0]
    yg = _rwkv(xs, batch, t_pad, row(w0[0]), w_decay_up[0].astype(BF16), row(a0[0]),
               w_aaa_up[0].astype(BF16), w_gate_up[0].astype(BF16), row(k_k[0]), row(k_a[0]), row(r_k[0]),
               row(ln_x_w[0]), row(ln_x_b[0]), ones_bd)

    seq3 = lambda a: a.reshape(batch, t_pad, a.shape[-1])
    return _out(x, seq3(yg), seq3(lru), seq3(yb), seq3(gates), w_proj_a[0].astype(BF16), w_proj_b[0].astype(BF16),
                w_out[0].astype(BF16), row(norm_ffn_g[0]), w_ff_up[0].astype(BF16), w_ff_down[0].astype(BF16),
                row(norm_final_g))
```
